```python
import jax, jax.numpy as jnp
from jax import lax
import numpy as np

D_MODEL = 1024
BATCH = 16
SEQ = 2048
DEPTH = 2
DEC_BATCH = 128
DEC_SEQ = 1
PAST_LEN = 16384
PAGE_SIZE = 128

EPS = 1e-6
Q_BLOCK = 128
NEG_INF = -1e30
GLA_HEADS = 4
GLA_DK = 32
GLA_DV = 64
GLA_GATE_RANK = 16
GLA_TAU = 16.0
GLA_CHUNK = 64
MLA_HEADS = 8
MLA_Q_RANK = 384
MLA_KV_RANK = 256
MLA_NOPE = 64
MLA_ROPE = 32
MLA_V = 64
ROPE_THETA = 10000.0
MLA_SCALE = (MLA_NOPE + MLA_ROPE) ** -0.5
FOX_HEADS = 4
FOX_KV_HEADS = 2
FOX_GROUP = FOX_HEADS // FOX_KV_HEADS
FOX_DH = 64
FOX_SCALE = FOX_DH ** -0.5
FORGET_BIAS_INIT = 3.0
MIX_WIDTH = GLA_HEADS * GLA_DV + MLA_HEADS * MLA_V + FOX_HEADS * FOX_DH
IN_SIZES = (GLA_HEADS * GLA_DK, GLA_HEADS * GLA_DK, GLA_HEADS * GLA_DV, GLA_GATE_RANK, GLA_HEADS * GLA_DV,
            MLA_Q_RANK, MLA_KV_RANK, MLA_ROPE,
            FOX_HEADS * FOX_DH, FOX_KV_HEADS * FOX_DH, FOX_KV_HEADS * FOX_DH, FOX_HEADS)
IN_WIDTH = sum(IN_SIZES)
N_EXPERTS = 16
N_GROUPS = 4
EXPERTS_PER_GROUP = N_EXPERTS // N_GROUPS
TOP_K = 2
D_EXPERT = 512
MOE_BLOCK = 128

kernel_name = 'hymba_gla_mla_fox_adaln_moe_step'


def rms_norm(x, gain):
    xf = x.astype(jnp.float32)
    y = xf * lax.rsqrt(jnp.mean(xf * xf, axis=-1, keepdims=True) + EPS)
    return (y * gain.astype(jnp.float32)).astype(x.dtype)


def adaln(c, w, b):
    return jnp.split(jax.nn.silu(c) @ w + b, 6, axis=-1)


def modulate(h, shift, scale):
    return h * (1.0 + scale[:, None]) + shift[:, None]


def rope(x, pos):
    half = x.shape[-1] // 2
    inv = ROPE_THETA ** (-jnp.arange(half, dtype=jnp.float32) / half)
    ang = pos.astype(jnp.float32)[:, None] * inv[None, :]
    cos, sin = jnp.cos(ang)[:, None], jnp.sin(ang)[:, None]
    x1, x2 = x[..., :half].astype(jnp.float32), x[..., half:].astype(jnp.float32)
    return jnp.concatenate([x1 * cos - x2 * sin, x1 * sin + x2 * cos], axis=-1).astype(x.dtype)


def mixer_inputs(h, pos, lw):
    B, L, _ = h.shape
    z = h @ lw['w_in']
    split_at = [int(v) for v in np.cumsum(IN_SIZES)[:-1]]
    gq, gk, gv, gg, gr, cq, ckv, kr, fq, fk, fv, ff = jnp.split(z, split_at, axis=-1)
    q = (rms_norm(cq, lw['mla_qlat_norm']) @ lw['mla_wuq']).reshape(B, L, MLA_HEADS, MLA_NOPE + MLA_ROPE)
    q = rms_norm(q, lw['mla_q_norm'])
    return {
        'gla_q': gq.reshape(B, L, GLA_HEADS, GLA_DK),
        'gla_k': gk.reshape(B, L, GLA_HEADS, GLA_DK),
        'gla_v': gv.reshape(B, L, GLA_HEADS, GLA_DV),
        'gla_gz': gg,
        'gla_r': gr,
        'mla_q_nope': q[..., :MLA_NOPE],
        'mla_q_rope': rope(q[..., MLA_NOPE:], pos),
        'mla_ckv': rms_norm(ckv, lw['mla_kv_norm']),
        'mla_krope': rope(rms_norm(kr, lw['mla_kr_norm'])[:, :, None], pos)[:, :, 0],
        'fox_q': rms_norm(fq.reshape(B, L, FOX_HEADS, FOX_DH), lw['fox_q_norm']).reshape(B, L, FOX_KV_HEADS, FOX_GROUP, FOX_DH),
        'fox_k': rms_norm(fk.reshape(B, L, FOX_KV_HEADS, FOX_DH), lw['fox_k_norm']),
        'fox_v': fv.reshape(B, L, FOX_KV_HEADS, FOX_DH),
        'fox_logf': jax.nn.log_sigmoid((ff + lw['fox_fb']).astype(jnp.float32)),
    }


def gla_mix(q, k, v, gz, r, s0, wa2, ba, onorm):
    B, L = q.shape[:2]
    f32 = jnp.float32
    log_a = jax.nn.log_sigmoid((gz @ wa2 + ba).astype(f32)).reshape(B, L, GLA_HEADS, GLA_DK) / GLA_TAU
    C = min(GLA_CHUNK, L)
    n = -(-L // C)
    pad = n * C - L

    def chunks(t):
        t = jnp.pad(t.astype(f32), [(0, 0), (0, pad), (0, 0), (0, 0)])
        return t.reshape(B, n, C, *t.shape[2:]).swapaxes(0, 1)

    causal = jnp.tril(jnp.ones((C, C), dtype=bool))

    def step(S, xs):
        qc, kc, vc, ac = xs
        b = jnp.cumsum(ac, axis=1)
        q_dec = qc * jnp.exp(b)
        k_inv = kc * jnp.exp(-b)
        o_inter = jnp.einsum('bchk,bhkv->bchv', q_dec, S)
        att = jnp.where(causal, jnp.einsum('bchk,bshk->bhcs', q_dec, k_inv), 0.0)
        o_intra = jnp.einsum('bhcs,bshv->bchv', att, vc)
        b_end = b[:, -1]
        S = S * jnp.exp(b_end)[..., None] + jnp.einsum('bshk,bshv->bhkv', kc * jnp.exp(b_end[:, None] - b), vc)
        return S, o_inter + o_intra

    s_fin, o = lax.scan(step, s0.astype(f32), (chunks(q * GLA_DK ** -0.5), chunks(k), chunks(v), chunks(log_a)))
    o = o.swapaxes(0, 1).reshape(B, n * C, GLA_HEADS, GLA_DV)[:, :L]
    o = rms_norm(o, onorm) * jax.nn.silu(r.astype(f32)).reshape(B, L, GLA_HEADS, GLA_DV)
    return o.reshape(B, L, GLA_HEADS * GLA_DV).astype(q.dtype), s_fin.astype(s0.dtype)


def block_causal_attention(q, k, v, scale, f_cum):
    B, S = q.shape[:2]
    kpos = jnp.arange(S)
    f_key = None if f_cum is None else f_cum.transpose(0, 2, 3, 1)[:, :, :, None, :]

    def one_block(i):
        start = i * Q_BLOCK
        qb = lax.dynamic_slice_in_dim(q, start, Q_BLOCK, axis=1)
        s = jnp.einsum('bqngd,bknd->bngqk', qb, k).astype(jnp.float32) * scale
        if f_cum is not None:
            f_q = lax.dynamic_slice_in_dim(f_cum, start, Q_BLOCK, axis=1).transpose(0, 2, 3, 1)
            s = s + f_q[..., None] - f_key
        qpos = start + jnp.arange(Q_BLOCK)
        s = jnp.where(kpos[None, :] <= qpos[:, None], s, -jnp.inf)
        p = jax.nn.softmax(s, axis=-1)
        return jnp.einsum('bngqk,bknd->bqngd', p.astype(v.dtype), v)

    out = lax.map(one_block, jnp.arange(S // Q_BLOCK))
    return out.swapaxes(0, 1).reshape(B, S, *out.shape[3:])


def _merge(carry, s, pv):
    m, den, acc = carry
    m_new = jnp.maximum(m, jnp.max(s, axis=-1))
    corr = jnp.exp(m - m_new)
    p = jnp.exp(s - m_new[..., None])
    return m_new, den * corr + jnp.sum(p, axis=-1), acc * corr[..., None] + pv(p)


def paged_attention(page_fn, xs, new_s, new_pv, dv):
    stat = new_s.shape[:-1]
    init = (jnp.full(stat, NEG_INF, jnp.float32), jnp.zeros(stat, jnp.float32), jnp.zeros(stat + (dv,), jnp.float32))

    def body(carry, x):
        s, pv = page_fn(*x)
        return _merge(carry, s, pv), None

    carry, _ = lax.scan(body, init, xs)
    _, den, acc = _merge(carry, new_s, new_pv)
    return acc / den[..., None]


def prompt_mixer(h, pos, lw):
    B, L, _ = h.shape
    m = mixer_inputs(h, pos, lw)
    s0 = jnp.zeros((B, GLA_HEADS, GLA_DK, GLA_DV), h.dtype)
    o_gla, s_gla = gla_mix(m['gla_q'], m['gla_k'], m['gla_v'], m['gla_gz'], m['gla_r'], s0,
                           lw['gla_wa2'], lw['gla_ba'], lw['gla_onorm'])
    k_nope = jnp.einsum('blc,chd->blhd', m['mla_ckv'], lw['mla_wuk'])
    v_mla = jnp.einsum('blc,chd->blhd', m['mla_ckv'], lw['mla_wuv'])
    q_mla = jnp.concatenate([m['mla_q_nope'], m['mla_q_rope']], axis=-1)[:, :, :, None]
    k_mla = jnp.concatenate([k_nope, jnp.broadcast_to(m['mla_krope'][:, :, None], (B, L, MLA_HEADS, MLA_ROPE))], axis=-1)
    o_mla = block_causal_attention(q_mla, k_mla, v_mla, MLA_SCALE, None).reshape(B, L, MLA_HEADS * MLA_V)
    f_cum = jnp.cumsum(m['fox_logf'], axis=1).reshape(B, L, FOX_KV_HEADS, FOX_GROUP)
    o_fox = block_causal_attention(m['fox_q'], m['fox_k'], m['fox_v'], FOX_SCALE, f_cum).reshape(B, L, FOX_HEADS * FOX_DH)
    out = jnp.concatenate([o_gla, o_mla.astype(h.dtype), o_fox.astype(h.dtype)], axis=-1) @ lw['w_out']
    return out, (m['mla_ckv'], m['mla_krope'], m['fox_k'], m['fox_v'], m['fox_logf'], s_gla)


def sample_mixer(h, pos, lw, layer, cache_mla_ckv, cache_mla_krope, cache_fox_k, cache_fox_v, cache_fox_logf,
                 page_table, s0):
    B, L, _ = h.shape
    f32 = jnp.float32
    m = mixer_inputs(h, pos, lw)
    causal = jnp.tril(jnp.ones((L, L), dtype=bool))
    o_gla, s_gla = gla_mix(m['gla_q'], m['gla_k'], m['gla_v'], m['gla_gz'], m['gla_r'], s0,
                           lw['gla_wa2'], lw['gla_ba'], lw['gla_onorm'])

    q_lat = jnp.einsum('bqhd,chd->bhqc', m['mla_q_nope'], lw['mla_wuk'])[:, :, None].astype(f32)
    q_rot = m['mla_q_rope'].transpose(0, 2, 1, 3)[:, :, None].astype(f32)

    def mla_scores(ckv, krope):
        return (jnp.einsum('bngqc,bkc->bngqk', q_lat, ckv) + jnp.einsum('bngqr,bkr->bngqk', q_rot, krope)) * MLA_SCALE

    def mla_page(phys):
        ckv = cache_mla_ckv[layer, phys].astype(f32)
        kr = cache_mla_krope[layer, phys].astype(f32)
        return mla_scores(ckv, kr), lambda p: jnp.einsum('bngqk,bkc->bngqc', p, ckv)

    ckv_new = m['mla_ckv'].astype(f32)
    s_new = jnp.where(causal, mla_scores(ckv_new, m['mla_krope'].astype(f32)), NEG_INF)
    o_lat = paged_attention(mla_page, (page_table.T,), s_new,
                            lambda p: jnp.einsum('bngqk,bkc->bngqc', p, ckv_new), MLA_KV_RANK)
    o_mla = jnp.einsum('bhqc,chd->bqhd', o_lat[:, :, 0], lw['mla_wuv'].astype(f32)).reshape(B, L, MLA_HEADS * MLA_V)

    n_pages = page_table.shape[1]
    lf_past = cache_fox_logf[layer, page_table].astype(f32).reshape(B, n_pages * PAGE_SIZE, FOX_HEADS)
    r_past = lax.cumsum(lf_past, axis=1, reverse=True) - lf_past
    r_pages = r_past.reshape(B, n_pages, PAGE_SIZE, FOX_KV_HEADS, FOX_GROUP).transpose(1, 0, 3, 4, 2)
    g_new = jnp.cumsum(m['fox_logf'], axis=1).reshape(B, L, FOX_KV_HEADS, FOX_GROUP).transpose(0, 2, 3, 1)
    q_fox = m['fox_q'].astype(f32)

    def fox_page(phys, r_blk):
        kp = cache_fox_k[layer, phys].astype(f32)
        vp = cache_fox_v[layer, phys].astype(f32)
        s = jnp.einsum('bqngd,bknd->bngqk', q_fox, kp) * FOX_SCALE + r_blk[:, :, :, None, :] + g_new[..., None]
        return s, lambda p: jnp.einsum('bngqk,bknd->bngqd', p, vp)

    k_new = m['fox_k'].astype(f32)
    v_new = m['fox_v'].astype(f32)
    s_new = jnp.einsum('bqngd,bknd->bngqk', q_fox, k_new) * FOX_SCALE + g_new[..., :, None] - g_new[..., None, :]
    s_new = jnp.where(causal, s_new, NEG_INF)
    o_fox = paged_attention(fox_page, (page_table.T, r_pages), s_new,
                            lambda p: jnp.einsum('bngqk,bknd->bngqd', p, v_new), FOX_DH)
    o_fox = o_fox.transpose(0, 3, 1, 2, 4).reshape(B, L, FOX_HEADS * FOX_DH)

    out = jnp.concatenate([o_gla, o_mla.astype(h.dtype), o_fox.astype(h.dtype)], axis=-1) @ lw['w_out']
    return out, (m['mla_ckv'], m['mla_krope'], m['fox_k'], m['fox_v'], m['fox_logf'], s_gla)


def route(t, router_w, router_b):
    T = t.shape[0]
    scores = jax.nn.sigmoid((t @ router_w).astype(jnp.float32))
    biased = (scores + router_b.astype(jnp.float32)).reshape(T, N_GROUPS, EXPERTS_PER_GROUP)
    g_sel = jnp.argmax(lax.top_k(biased, TOP_K)[0].sum(-1), axis=-1)
    _, local = lax.top_k(biased[jnp.arange(T), g_sel], TOP_K)
    idx = g_sel[:, None] * EXPERTS_PER_GROUP + local
    w = jnp.take_along_axis(scores, idx, axis=1)
    return idx, w / jnp.sum(w, axis=-1, keepdims=True)


def moe_dispatch(t, idx, w, w1, w3, w2):
    T, D = t.shape
    flat_e = idx.reshape(-1)
    TK = flat_e.shape[0]
    order = jnp.argsort(flat_e)
    e_sorted = flat_e[order]
    counts = jnp.bincount(flat_e, length=N_EXPERTS)
    start = jnp.cumsum(counts) - counts
    padded = (counts + MOE_BLOCK - 1) // MOE_BLOCK * MOE_BLOCK
    pend = jnp.cumsum(padded)
    pstart = pend - padded
    dest = pstart[e_sorted] + jnp.arange(TK) - start[e_sorted]
    n_blocks = -(-TK // MOE_BLOCK) + N_EXPERTS
    n_slots = n_blocks * MOE_BLOCK
    slot_tok = jnp.full((n_slots,), T, jnp.int32).at[dest].set((order // TOP_K).astype(jnp.int32))
    slot_w = jnp.zeros((n_slots,), t.dtype).at[dest].set(w.reshape(-1)[order].astype(t.dtype))
    block_e = jnp.clip(jnp.searchsorted(pend, jnp.arange(n_blocks) * MOE_BLOCK, side='right'), 0, N_EXPERTS - 1)
    xb = jnp.concatenate([t, jnp.zeros((1, D), t.dtype)], axis=0)[slot_tok].reshape(n_blocks, MOE_BLOCK, D)

    def expert_block(args):
        xe, e = args
        return (jax.nn.silu(xe @ w1[e]) * (xe @ w3[e])) @ w2[e]

    yb = lax.map(expert_block, (xb, block_e)).reshape(n_slots, D)
    out = jnp.zeros((T + 1, D), t.dtype).at[slot_tok].add(yb * slot_w[:, None])
    return out[:T]


def channel_mixer(h, router_w, router_b, w1, w3, w2):
    B, L, D = h.shape
    t = h.reshape(B * L, D)
    idx, w = route(t, router_w, router_b)
    return moe_dispatch(t, idx, w, w1, w3, w2).reshape(B, L, D)


def _stack(entries, i):
    return jnp.stack([e[i] for e in entries], axis=0)


def setup_inputs(seed: int = 0) -> dict:
    key = jax.random.key(seed)
    keys = jax.random.split(key, 40)
    counter = [0]

    def nrm(shape, scale=1.0):
        k = keys[counter[0]]
        counter[0] += 1
        return scale * jax.random.normal(k, shape, jnp.float32)

    def gain(shape):
        return 1.0 + nrm(shape, 0.1)

    d = D_MODEL
    n_pages = PAST_LEN // PAGE_SIZE
    n_pool = DEC_BATCH * n_pages * 5 // 4
    perm = jax.random.permutation(keys[39], n_pool)
    page_table = perm[:DEC_BATCH * n_pages].reshape(DEC_BATCH, n_pages).astype(jnp.int32)
    return {
        'x_prompt': nrm((BATCH, SEQ, d)),
        'x_sample': nrm((DEC_BATCH, DEC_SEQ, d)),
        'cache_mla_ckv': nrm((DEPTH, n_pool, PAGE_SIZE, MLA_KV_RANK)),
        'cache_mla_krope': nrm((DEPTH, n_pool, PAGE_SIZE, MLA_ROPE)),
        'cache_fox_k': nrm((DEPTH, n_pool, PAGE_SIZE, FOX_KV_HEADS, FOX_DH)),
        'cache_fox_v': nrm((DEPTH, n_pool, PAGE_SIZE, FOX_KV_HEADS, FOX_DH)),
        'cache_fox_logf': jax.nn.log_sigmoid(FORGET_BIAS_INIT + nrm((DEPTH, n_pool, PAGE_SIZE, FOX_HEADS))),
        'state_gla': nrm((DEPTH, DEC_BATCH, GLA_HEADS, GLA_DK, GLA_DV)),
        'page_table': page_table,
        'c_prompt': nrm((BATCH, d)),
        'c_sample': nrm((DEC_BATCH, d)),
        'router_w': nrm((d, N_EXPERTS), d ** -0.5),
        'router_b': nrm((N_EXPERTS,), 0.01),
        'attn_norm': gain((DEPTH, d)),
        'ffn_norm': gain((DEPTH, d)),
        'ada_w': nrm((DEPTH, d, 6 * d), 0.5 * d ** -0.5),
        'ada_b': nrm((DEPTH, 6 * d), 0.02),
        'w_in': nrm((DEPTH, d, IN_WIDTH), d ** -0.5),
        'gla_wa2': nrm((DEPTH, GLA_GATE_RANK, GLA_HEADS * GLA_DK), GLA_GATE_RANK ** -0.5),
        'gla_ba': nrm((DEPTH, GLA_HEADS * GLA_DK), 0.1),
        'gla_onorm': gain((DEPTH, GLA_DV)),
        'mla_qlat_norm': gain((DEPTH, MLA_Q_RANK)),
        'mla_wuq': nrm((DEPTH, MLA_Q_RANK, MLA_HEADS * (MLA_NOPE + MLA_ROPE)), MLA_Q_RANK ** -0.5),
        'mla_q_norm': gain((DEPTH, MLA_NOPE + MLA_ROPE)),
        'mla_kv_norm': gain((DEPTH, MLA_KV_RANK)),
        'mla_kr_norm': gain((DEPTH, MLA_ROPE)),
        'mla_wuk': nrm((DEPTH, MLA_KV_RANK, MLA_HEADS, MLA_NOPE), MLA_KV_RANK ** -0.5),
        'mla_wuv': nrm((DEPTH, MLA_KV_RANK, MLA_HEADS, MLA_V), MLA_KV_RANK ** -0.5),
        'fox_q_norm': gain((DEPTH, FOX_DH)),
        'fox_k_norm': gain((DEPTH, FOX_DH)),
        'fox_fb': FORGET_BIAS_INIT + nrm((DEPTH, FOX_HEADS), 0.1),
        'w_out': nrm((DEPTH, MIX_WIDTH, d), MIX_WIDTH ** -0.5),
        'moe_w1': nrm((DEPTH, N_EXPERTS, d, D_EXPERT), d ** -0.5),
        'moe_w3': nrm((DEPTH, N_EXPERTS, d, D_EXPERT), d ** -0.5),
        'moe_w2': nrm((DEPTH, N_EXPERTS, D_EXPERT, d), D_EXPERT ** -0.5),
    }


def reference(x_prompt, x_sample, cache_mla_ckv, cache_mla_krope, cache_fox_k, cache_fox_v, cache_fox_logf,
              state_gla, page_table, c_prompt, c_sample, router_w, router_b, attn_norm, ffn_norm, ada_w, ada_b,
              w_in, gla_wa2, gla_ba, gla_onorm, mla_qlat_norm, mla_wuq, mla_q_norm, mla_kv_norm, mla_kr_norm,
              mla_wuk, mla_wuv, fox_q_norm, fox_k_norm, fox_fb, w_out, moe_w1, moe_w3, moe_w2):
    xp, xs = x_prompt, x_sample
    pos_p = jnp.arange(xp.shape[1])
    pos_s = page_table.shape[1] * PAGE_SIZE + jnp.arange(xs.shape[1])
    p_ent, s_ent = [], []
    for layer in range(DEPTH):
        lw = {
            'w_in': w_in[layer], 'w_out': w_out[layer],
            'gla_wa2': gla_wa2[layer], 'gla_ba': gla_ba[layer], 'gla_onorm': gla_onorm[layer],
            'mla_qlat_norm': mla_qlat_norm[layer], 'mla_wuq': mla_wuq[layer], 'mla_q_norm': mla_q_norm[layer],
            'mla_kv_norm': mla_kv_norm[layer], 'mla_kr_norm': mla_kr_norm[layer],
            'mla_wuk': mla_wuk[layer], 'mla_wuv': mla_wuv[layer],
            'fox_q_norm': fox_q_norm[layer], 'fox_k_norm': fox_k_norm[layer], 'fox_fb': fox_fb[layer],
        }
        mod_p = adaln(c_prompt, ada_w[layer], ada_b[layer])
        mod_s = adaln(c_sample, ada_w[layer], ada_b[layer])
        o, e = prompt_mixer(modulate(rms_norm(xp, attn_norm[layer]), mod_p[0], mod_p[1]), pos_p, lw)
        xp = xp + mod_p[2][:, None] * o
        p_ent.append(e)
        o, e = sample_mixer(modulate(rms_norm(xs, attn_norm[layer]), mod_s[0], mod_s[1]), pos_s, lw, layer,
                            cache_mla_ckv, cache_mla_krope, cache_fox_k, cache_fox_v, cache_fox_logf,
                            page_table, state_gla[layer])
        xs = xs + mod_s[2][:, None] * o
        s_ent.append(e)
        xp = xp + mod_p[5][:, None] * channel_mixer(modulate(rms_norm(xp, ffn_norm[layer]), mod_p[3], mod_p[4]),
                                                    router_w, router_b, moe_w1[layer], moe_w3[layer], moe_w2[layer])
        xs = xs + mod_s[5][:, None] * channel_mixer(modulate(rms_norm(xs, ffn_norm[layer]), mod_s[3], mod_s[4]),
                                                    router_w, router_b, moe_w1[layer], moe_w3[layer], moe_w2[layer])
    return (xp, xs,
            _stack(p_ent, 0), _stack(p_ent, 1), _stack(p_ent, 2), _stack(p_ent, 3), _stack(p_ent, 4), _stack(p_ent, 5),
            _stack(s_ent, 0), _stack(s_ent, 1), _stack(s_ent, 2), _stack(s_ent, 3), _stack(s_ent, 4), _stack(s_ent, 5))
```

```python
import functools

import numpy as np
import jax
import jax.numpy as jnp
from jax import lax
from jax.experimental import pallas as pl
from jax.experimental.pallas import tpu as pltpu

F32, BF16, I32, U32 = jnp.float32, jnp.bfloat16, jnp.int32, jnp.uint32

D_MODEL = 1024
EPS = 1e-6
GLA_HEADS, GLA_DK, GLA_DV, GLA_RANK, GLA_TAU, GLA_CHUNK = 4, 32, 64, 16, 16.0, 64
MLA_HEADS, MLA_Q_RANK, MLA_KV_RANK, MLA_NOPE, MLA_ROPE, MLA_V = 8, 384, 256, 64, 32, 64
MLA_QK = MLA_NOPE + MLA_ROPE
MLA_SCALE = MLA_QK ** -0.5
ROPE_THETA = 10000.0
FOX_HEADS, FOX_KV_HEADS, FOX_GROUP, FOX_DH = 4, 2, 2, 64
FOX_SCALE = FOX_DH ** -0.5
N_EXPERTS, N_GROUPS, EXPERTS_PER_GROUP, D_EXPERT = 16, 4, 4, 512
PAGE_SIZE = 128
IN_SIZES = (128, 128, 256, 16, 256, 384, 256, 32, 256, 128, 128, 4)
ATT_HEADS = MLA_HEADS + FOX_HEADS

LANES = 128
VMEM_LIMIT = 56 * 1024 * 1024

O_GQ, O_GK, O_GV, O_GR, O_CQ, O_CKV, O_FQ, O_FK, O_FV, O_SM = 0, 128, 256, 512, 768, 1152, 1408, 1920, 2176, 2432
W_IN_PACKED = 2560
SM_GG, SM_FF, SM_KR = 0, 16, 64
GLA_PACK = 896

TM_PROMPT = 512
TQ = 512
GLA_SUPER = 512
MOE_BLK = 256
PAGES_PER_STEP = 16


def _cparams(n_axes):
    return pltpu.CompilerParams(dimension_semantics=("arbitrary",) * n_axes, vmem_limit_bytes=VMEM_LIMIT)


def _dot(a, b):
    return jnp.dot(a, b, preferred_element_type=F32)


def _dot_nt(a, b):
    return lax.dot_general(a, b, (((1,), (1,)), ((), ())), preferred_element_type=F32)


def _dot_tn(a, b):
    return lax.dot_general(a, b, (((0,), (0,)), ((), ())), preferred_element_type=F32)


def _split(x, n):
    out, r = [], x
    for _ in range(n):
        p = r.astype(BF16)
        out.append(p)
        r = r - p.astype(F32)
    return out


def _dot_split(a_exact, b, n):
    acc = None
    for p in _split(b, n):
        t = _dot(a_exact, p)
        acc = t if acc is None else acc + t
    return acc


def _logsig(x):
    return jnp.minimum(x, 0.0) - jnp.log1p(jnp.exp(-jnp.abs(x)))


def _silu(x):
    return x / (1.0 + jnp.exp(-x))


def _rms(x, gain):
    return x * lax.rsqrt(jnp.mean(x * x, axis=-1, keepdims=True) + EPS) * gain


def _pack_rows(x):
    hi = pltpu.bitcast(x[:, :512].astype(BF16).astype(F32), U32)
    lo = pltpu.bitcast(x[:, 512:].astype(BF16).astype(F32), U32) >> 16
    return hi | lo


def _unpack_rows(w):
    return pltpu.bitcast(w & jnp.uint32(0xFFFF0000), F32), pltpu.bitcast(w << 16, F32)


def _ada_kernel(c_ref, w_ref, b_ref, o_ref):
    s_hi, s_lo = _split(_silu(c_ref[...]), 2)
    w_hi, w_lo = _split(w_ref[0], 2)
    o_ref[0] = _dot(s_hi, w_hi) + _dot(s_hi, w_lo) + _dot(s_lo, w_hi) + b_ref[0]


def _ada_call(c_all, ada_w, ada_b):
    rows, depth, n = c_all.shape[0], ada_w.shape[0], ada_w.shape[2]
    tn = n // 4
    return pl.pallas_call(
        _ada_kernel, grid=(depth, n // tn),
        in_specs=[pl.BlockSpec((rows, D_MODEL), lambda l, j: (0, 0)),
                  pl.BlockSpec((1, D_MODEL, tn), lambda l, j: (l, 0, j)),
                  pl.BlockSpec((1, 1, tn), lambda l, j: (l, 0, j))],
        out_specs=pl.BlockSpec((1, rows, tn), lambda l, j: (l, 0, j)),
        out_shape=jax.ShapeDtypeStruct((depth, rows, n), F32),
        compiler_params=_cparams(2), name="ada_mod",
    )(c_all, ada_w, ada_b.reshape(depth, 1, n))


def _premix_kernel(x_ref, sh_ref, sc_ref, cos_ref, sin_ref, gat_ref, win_ref, gql_ref, wuq_ref, gqp_ref, gkv_ref,
                   wuk_ref, wuv_ref, gkr_ref, gfq_ref, gfk_ref, fb_ref, wa2_ref, ba_ref, epl_ref, oneq_ref, onek_ref,
                   qall_ref, kall_ref, vall_ref, ckv_ref, kr_ref, fk_ref, fv_ref, lf_ref, gla_ref, carry_ref,
                   *, tiles_per_seq, per_token):
    i = pl.program_id(0)
    tm = x_ref.shape[0]
    x = x_ref[...]
    sh, sc = (sh_ref[...], sc_ref[...]) if per_token else (sh_ref[0], sc_ref[0])
    h = _rms(x, gat_ref[...]) * (1.0 + sc) + sh
    z = _dot(h.astype(BF16), win_ref[...])

    lane = lax.broadcasted_iota(I32, (tm, LANES), 1)
    cos, sin = cos_ref[...], sin_ref[...]

    def rope(v):
        partner = jnp.where(lane < 80, pltpu.roll(v, 112, 1), pltpu.roll(v, 16, 1))
        return v * cos + partner * sin

    small = z[:, O_SM:O_SM + LANES]

    gate = _dot(small.astype(BF16), wa2_ref[...]) + ba_ref[...]
    gla_ref[:, 0:128] = z[:, O_GQ:O_GQ + 128] * (GLA_DK ** -0.5)
    gla_ref[:, 128:256] = z[:, O_GK:O_GK + 128]
    gla_ref[:, 256:384] = _logsig(gate) * (1.0 / GLA_TAU)
    gla_ref[:, 384:640] = z[:, O_GV:O_GV + 256]
    gla_ref[:, 640:896] = z[:, O_GR:O_GR + 256]

    ckvn = _rms(z[:, O_CKV:O_CKV + MLA_KV_RANK], gkv_ref[...])
    ckv_ref[...] = ckvn
    ckvb = ckvn.astype(BF16)
    knope = _dot(ckvb, wuk_ref[...])
    vall_ref[:, 0:512] = _dot(ckvb, wuv_ref[...]).astype(BF16)
    krm = jnp.where((lane >= SM_KR) & (lane < SM_KR + MLA_ROPE), small, 0.0)
    krn = krm * lax.rsqrt(jnp.sum(krm * krm, axis=-1, keepdims=True) * (1.0 / MLA_ROPE) + EPS) * gkr_ref[...]
    kro = rope(krn)
    kr_ref[...] = kro[:, SM_KR:SM_KR + MLA_ROPE]

    q = _dot(_rms(z[:, O_CQ:O_CQ + MLA_Q_RANK], gql_ref[...]).astype(BF16), wuq_ref[...])
    for hd in range(MLA_HEADS):
        sl = slice(hd * LANES, (hd + 1) * LANES)
        qh = q[:, sl]
        ms = jnp.sum(qh * qh, axis=-1, keepdims=True) * (1.0 / MLA_QK)
        qh = rope(qh * lax.rsqrt(ms + EPS) * gqp_ref[...])
        qall_ref[:, sl] = (qh * MLA_SCALE).astype(BF16)
        kall_ref[:, sl] = (knope[:, sl] + kro).astype(BF16)

    kn = []
    for n in range(FOX_KV_HEADS):
        blk = z[:, O_FK + n * LANES:O_FK + (n + 1) * LANES]
        ms = jnp.sum(blk * blk, axis=-1, keepdims=True) * (1.0 / LANES)
        kn.append(blk * lax.rsqrt(ms + EPS) * gfk_ref[...])
    fk_ref[...] = jnp.where(lane < FOX_DH, kn[0], kn[1])
    fvd = z[:, O_FV:O_FV + 256]
    fv_ref[...] = jnp.where(lane < FOX_DH, fvd[:, 0:128], fvd[:, 128:256])
    vall_ref[:, 512:768] = fvd.astype(BF16)

    lfv = jnp.where((lane >= SM_FF) & (lane < SM_FF + FOX_HEADS), _logsig(small + fb_ref[...]), 0.0)
    lf_ref[...] = lfv[:, SM_FF:SM_FF + FOX_HEADS]

    if per_token:
        augq = augk = None
    else:
        @pl.when(i % tiles_per_seq == 0)
        def _():
            carry_ref[...] = jnp.zeros_like(carry_ref)
        row = lax.broadcasted_iota(I32, (tm, tm), 0)
        col = lax.broadcasted_iota(I32, (tm, tm), 1)
        tril = jnp.where(row >= col, 1.0, 0.0).astype(BF16)
        fcum = _dot_split(tril, lfv, 3) + carry_ref[0:1, :]
        carry_ref[0:1, :] = fcum[tm - 1:tm, :]
        aug = _dot(jnp.concatenate(_split(fcum, 3), axis=1), epl_ref[...])
        augq = aug[:, 0:512] + oneq_ref[...]
        augk = aug[:, 512:1024] + onek_ref[...]

    for hd in range(FOX_HEADS):
        blk = z[:, O_FQ + hd * LANES:O_FQ + (hd + 1) * LANES]
        ms = jnp.sum(blk * blk, axis=-1, keepdims=True) * (1.0 / FOX_DH)
        qh = blk * lax.rsqrt(ms + EPS) * gfq_ref[...] * FOX_SCALE
        kh = jnp.where(lane < FOX_DH, kn[hd // FOX_GROUP], 0.0)
        if augq is not None:
            qh = qh + augq[:, hd * LANES:(hd + 1) * LANES]
            kh = kh + augk[:, hd * LANES:(hd + 1) * LANES]
        sl = slice((MLA_HEADS + hd) * LANES, (MLA_HEADS + hd + 1) * LANES)
        qall_ref[:, sl] = qh.astype(BF16)
        kall_ref[:, sl] = kh.astype(BF16)


def _rope_tables(pos):
    half = MLA_ROPE // 2
    inv = ROPE_THETA ** (-jnp.arange(half, dtype=F32) / half)
    ang = pos.astype(F32)[:, None] * inv[None, :]
    c, s = jnp.cos(ang), jnp.sin(ang)
    n = pos.shape[0]
    cos_t = jnp.concatenate([jnp.ones((n, 64), F32), c, c, jnp.zeros((n, 32), F32)], axis=1)
    sin_t = jnp.concatenate([jnp.zeros((n, 64), F32), -s, s, jnp.zeros((n, 32), F32)], axis=1)
    return cos_t, sin_t


def _lane_vec(pieces):
    v = jnp.zeros((LANES,), F32)
    for off, val in pieces:
        v = lax.dynamic_update_slice(v, val.astype(F32), (off,))
    return v[None, :]


def _aug_constants():
    e = np.zeros((3 * LANES, 2 * FOX_HEADS * LANES), np.float32)
    oneq = np.zeros((1, FOX_HEADS * LANES), np.float32)
    onek = np.zeros((1, FOX_HEADS * LANES), np.float32)
    for hd in range(FOX_HEADS):
        for j in range(3):
            e[j * LANES + SM_FF + hd, hd * LANES + FOX_DH + j] = 1.0
            e[j * LANES + SM_FF + hd, FOX_HEADS * LANES + hd * LANES + FOX_DH + 3 + j] = -1.0
            oneq[0, hd * LANES + FOX_DH + 3 + j] = 1.0
            onek[0, hd * LANES + FOX_DH + j] = 1.0
    return jnp.asarray(e, BF16), jnp.asarray(oneq), jnp.asarray(onek)


def _layer_weights(layer, w_in, gla_wa2, gla_ba, gla_onorm, mla_qlat_norm, mla_wuq, mla_q_norm, mla_kv_norm,
                   mla_kr_norm, mla_wuk, mla_wuv, fox_q_norm, fox_k_norm, fox_fb, attn_norm):
    w = w_in[layer]
    offs = np.concatenate([[0], np.cumsum(IN_SIZES)])
    gq, gk, gv, gg, gr, cq, ckv, kr, fq, fk, fv, ff = [w[:, offs[j]:offs[j + 1]] for j in range(12)]
    z64 = jnp.zeros((D_MODEL, 64), F32)
    fq_p = jnp.concatenate([t for hd in range(FOX_HEADS) for t in (fq[:, hd * 64:(hd + 1) * 64], z64)], axis=1)
    fk_p = jnp.concatenate([fk[:, 0:64], fk[:, 0:64], fk[:, 64:128], fk[:, 64:128]], axis=1)
    fv_p = jnp.concatenate([fv[:, 0:64], fv[:, 0:64], fv[:, 64:128], fv[:, 64:128]], axis=1)
    sm = jnp.concatenate([gg, ff, jnp.zeros((D_MODEL, SM_KR - SM_FF - FOX_HEADS), F32), kr,
                          jnp.zeros((D_MODEL, LANES - SM_KR - MLA_ROPE), F32)], axis=1)
    win_p = jnp.concatenate([gq, gk, gv, gr, cq, ckv, fq_p, fk_p, fv_p, sm], axis=1).astype(BF16)

    wuq = mla_wuq[layer].reshape(MLA_Q_RANK, MLA_HEADS, MLA_QK)
    wuq_p = jnp.concatenate([wuq, jnp.zeros((MLA_Q_RANK, MLA_HEADS, LANES - MLA_QK), F32)], axis=2)
    wuq_p = wuq_p.reshape(MLA_Q_RANK, MLA_HEADS * LANES).astype(BF16)
    wuk = mla_wuk[layer]
    wuk_p = jnp.concatenate([wuk, jnp.zeros_like(wuk)], axis=2).reshape(MLA_KV_RANK, MLA_HEADS * LANES).astype(BF16)
    wuv_p = mla_wuv[layer].reshape(MLA_KV_RANK, MLA_HEADS * MLA_V).astype(BF16)
    wa2_p = jnp.zeros((LANES, LANES), F32).at[SM_GG:SM_GG + GLA_RANK].set(gla_wa2[layer]).astype(BF16)
    return dict(
        win=win_p, wuq=wuq_p, wuk=wuk_p, wuv=wuv_p, wa2=wa2_p,
        gat=attn_norm[layer][None, :], gql=mla_qlat_norm[layer][None, :], gkv=mla_kv_norm[layer][None, :],
        gqp=_lane_vec([(0, mla_q_norm[layer])]),
        gkr=_lane_vec([(SM_KR, mla_kr_norm[layer])]),
        gfq=_lane_vec([(0, fox_q_norm[layer])]),
        gfk=_lane_vec([(0, fox_k_norm[layer]), (FOX_DH, fox_k_norm[layer])]),
        fb=_lane_vec([(SM_FF, fox_fb[layer])]),
        ba=gla_ba[layer][None, :],
        onorm=jnp.tile(gla_onorm[layer], GLA_HEADS)[None, :],
        wuk_raw=wuk, wuv_raw=mla_wuv[layer],
    )


def _premix_call(x2d, shift, scale, cos_t, sin_t, lw, aug, *, tm, tiles_per_seq, per_token):
    t = x2d.shape[0]
    nt = t // tm
    epl, oneq, onek = aug
    full = lambda a: pl.BlockSpec(a.shape, lambda i: (0,) * a.ndim)
    tok = lambda w: pl.BlockSpec((tm, w), lambda i: (i, 0))
    if per_token:
        mod_spec = tok(D_MODEL)
        pos_spec = full(cos_t)
    else:
        mod_spec = pl.BlockSpec((1, 1, D_MODEL), lambda i: (i // tiles_per_seq, 0, 0))
        pos_spec = pl.BlockSpec((tm, LANES), lambda i: (i % tiles_per_seq, 0))
    consts = [lw["gat"], lw["win"], lw["gql"], lw["wuq"], lw["gqp"], lw["gkv"], lw["wuk"], lw["wuv"], lw["gkr"],
              lw["gfq"], lw["gfk"], lw["fb"], lw["wa2"], lw["ba"], epl, oneq, onek]
    out_widths = [(ATT_HEADS * LANES, BF16), (ATT_HEADS * LANES, BF16), (ATT_HEADS * 64, BF16), (MLA_KV_RANK, F32),
                  (MLA_ROPE, F32), (LANES, F32), (LANES, F32), (FOX_HEADS, F32), (GLA_PACK, F32)]
    return pl.pallas_call(
        functools.partial(_premix_kernel, tiles_per_seq=tiles_per_seq, per_token=per_token),
        grid=(nt,),
        in_specs=[tok(D_MODEL), mod_spec, mod_spec, pos_spec, pos_spec] + [full(c) for c in consts],
        out_specs=[tok(w) for w, _ in out_widths],
        out_shape=[jax.ShapeDtypeStruct((t, w), dt) for w, dt in out_widths],
        scratch_shapes=[pltpu.VMEM((8, LANES), F32)],
        compiler_params=_cparams(1), name="premix",
    )(x2d, shift, scale, cos_t, sin_t, *consts)


def _flash_kernel(q_ref, k_ref, v_ref, o_ref, *, tq):
    qi = pl.program_id(2)
    row = lax.broadcasted_iota(I32, (tq, tq), 0)
    col = lax.broadcasted_iota(I32, (tq, tq), 1)
    for hh in range(2):
        q = q_ref[0, :, hh * LANES:(hh + 1) * LANES]

        def step(kb, carry, diagonal):
            m, l, acc = carry
            start = pl.multiple_of(kb * tq, tq)
            k = k_ref[0, pl.ds(start, tq), hh * LANES:(hh + 1) * LANES]
            v = v_ref[0, pl.ds(start, tq), hh * 64:(hh + 1) * 64]
            s = _dot_nt(q, k)
            if diagonal:
                s = jnp.where(row >= col, s, -jnp.inf)
            m_new = jnp.maximum(m, jnp.max(s, axis=-1, keepdims=True))
            alpha = jnp.exp(m - m_new)
            p = jnp.exp(s - m_new)
            l = alpha * l + jnp.sum(p, axis=-1, keepdims=True)
            acc = alpha * acc + _dot(p.astype(BF16), v)
            return m_new, l, acc

        init = (jnp.full((tq, 1), -jnp.inf, F32), jnp.zeros((tq, 1), F32), jnp.zeros((tq, 64), F32))
        carry = lax.fori_loop(0, qi, lambda kb, c: step(kb, c, False), init)
        _, l, acc = step(qi, carry, True)
        o_ref[0, :, hh * 64:(hh + 1) * 64] = (acc / l).astype(BF16)


def _flash_call(q_all, k_all, v_all, *, tq):
    b, s, _ = q_all.shape
    return pl.pallas_call(
        functools.partial(_flash_kernel, tq=tq),
        grid=(b, ATT_HEADS // 2, s // tq),
        in_specs=[pl.BlockSpec((1, tq, 2 * LANES), lambda bi, p, qi: (bi, qi, p)),
                  pl.BlockSpec((1, s, 2 * LANES), lambda bi, p, qi: (bi, 0, p)),
                  pl.BlockSpec((1, s, LANES), lambda bi, p, qi: (bi, 0, p))],
        out_specs=pl.BlockSpec((1, tq, LANES), lambda bi, p, qi: (bi, qi, p)),
        out_shape=jax.ShapeDtypeStruct((b, s, ATT_HEADS * 64), BF16),
        compiler_params=_cparams(3), name="flash_attn",
    )(q_all, k_all, v_all)


def _gla_masks():
    r128 = lax.broadcasted_iota(I32, (GLA_HEADS * GLA_DK, GLA_HEADS * GLA_DV), 0)
    c256 = lax.broadcasted_iota(I32, (GLA_HEADS * GLA_DK, GLA_HEADS * GLA_DV), 1)
    return (r128 // GLA_DK) == (c256 // GLA_DV)


def _gla_out_norm(o, r, onorm):
    rr = lax.broadcasted_iota(I32, (256, 256), 0)
    cc = lax.broadcasted_iota(I32, (256, 256), 1)
    same = jnp.where((rr // GLA_DV) == (cc // GLA_DV), 1.0, 0.0).astype(BF16)
    ms = _dot((o * o).astype(BF16), same) * (1.0 / GLA_DV)
    return o * lax.rsqrt(ms + EPS) * onorm * _silu(r)


def _gla_kernel(g_ref, onorm_ref, o_ref, st_ref, sbd_ref, *, n_chunks):
    si = pl.program_id(1)

    @pl.when(si == 0)
    def _():
        sbd_ref[...] = jnp.zeros_like(sbd_ref)

    c = GLA_CHUNK
    lane128 = lax.broadcasted_iota(I32, (1, 128), 1)
    lane256 = lax.broadcasted_iota(I32, (1, 256), 1)
    bd = _gla_masks()
    rr = lax.broadcasted_iota(I32, (c, c), 0)
    cc = lax.broadcasted_iota(I32, (c, c), 1)
    tril = jnp.where(rr >= cc, 1.0, 0.0).astype(BF16)
    r4 = lax.broadcasted_iota(I32, (GLA_HEADS * c, c), 0)
    c4 = lax.broadcasted_iota(I32, (GLA_HEADS * c, c), 1)
    causal4 = (r4 % c) >= c4
    e_r = lax.broadcasted_iota(I32, (128, 128), 0)
    e_c = lax.broadcasted_iota(I32, (128, 128), 1)
    eye = e_r == e_c

    for ci in range(n_chunks):
        blk = g_ref[0, ci * c:(ci + 1) * c, :]
        q, k, la = blk[:, 0:128], blk[:, 128:256], blk[:, 256:384]
        v, r = blk[:, 384:640], blk[:, 640:896]
        b = _dot_split(tril, la, 2)
        b_end = b[c - 1:c, :]
        q_dec = q * jnp.exp(b)
        k_inv = (k * jnp.exp(-b)).astype(BF16)
        k_dec = (k * jnp.exp(b_end - b)).astype(BF16)
        vb = v.astype(BF16)
        s_bd = sbd_ref[...]
        o = _dot(q_dec.astype(BF16), s_bd.astype(BF16))
        q4 = jnp.concatenate([jnp.where((lane128 // GLA_DK) == hd, q_dec, 0.0) for hd in range(GLA_HEADS)], axis=0)
        att = jnp.where(causal4, _dot_nt(q4.astype(BF16), k_inv), 0.0)
        r_all = _dot(att.astype(BF16), vb)
        for hd in range(GLA_HEADS):
            o = o + jnp.where((lane256 // GLA_DV) == hd, r_all[hd * c:(hd + 1) * c, :], 0.0)
        o_ref[0, ci * c:(ci + 1) * c, :] = _gla_out_norm(o, r, onorm_ref[...]).astype(BF16)
        dcol = jnp.sum(jnp.where(eye, jnp.broadcast_to(jnp.exp(b_end), (128, 128)), 0.0), axis=1, keepdims=True)
        sbd_ref[...] = s_bd * dcol + jnp.where(bd, _dot_tn(k_dec, vb), 0.0)

    @pl.when(si == pl.num_programs(1) - 1)
    def _():
        s_bd = sbd_ref[...]
        for hd in range(GLA_HEADS):
            st_ref[0, hd] = s_bd[hd * GLA_DK:(hd + 1) * GLA_DK, hd * GLA_DV:(hd + 1) * GLA_DV]


def _gla_call(gla_in, onorm, *, sup):
    b, s, _ = gla_in.shape
    return pl.pallas_call(
        functools.partial(_gla_kernel, n_chunks=sup // GLA_CHUNK),
        grid=(b, s // sup),
        in_specs=[pl.BlockSpec((1, sup, GLA_PACK), lambda bi, si: (bi, si, 0)),
                  pl.BlockSpec((1, 256), lambda bi, si: (0, 0))],
        out_specs=[pl.BlockSpec((1, sup, 256), lambda bi, si: (bi, si, 0)),
                   pl.BlockSpec((1, GLA_HEADS, GLA_DK, GLA_DV), lambda bi, si: (bi, 0, 0, 0))],
        out_shape=[jax.ShapeDtypeStruct((b, s, 256), BF16),
                   jax.ShapeDtypeStruct((b, GLA_HEADS, GLA_DK, GLA_DV), F32)],
        scratch_shapes=[pltpu.VMEM((GLA_HEADS * GLA_DK, GLA_HEADS * GLA_DV), F32)],
        compiler_params=_cparams(2), name="gla_chunked",
    )(gla_in, onorm)


def _gla_step_kernel(g_ref, s0_ref, onorm_ref, o_ref, st_ref, *, nb):
    lane128 = lax.broadcasted_iota(I32, (1, 128), 1)
    lane256 = lax.broadcasted_iota(I32, (1, 256), 1)
    bd = _gla_masks()
    e_r = lax.broadcasted_iota(I32, (128, 128), 0)
    e_c = lax.broadcasted_iota(I32, (128, 128), 1)
    eye = e_r == e_c
    rows = []
    for j in range(nb):
        blk = g_ref[j:j + 1, :]
        q, k, la = blk[:, 0:128], blk[:, 128:256], blk[:, 256:384]
        v = blk[:, 384:640]
        s0 = s0_ref[j]
        s_st = s0.reshape(GLA_HEADS * GLA_DK, GLA_DV)
        s_bd = jnp.where(bd, jnp.concatenate([s_st] * GLA_HEADS, axis=1), 0.0)
        ea = jnp.exp(la)
        o = _dot((q * ea).astype(BF16), s_bd.astype(BF16))
        qk = q * k
        for hd in range(GLA_HEADS):
            dot_h = jnp.sum(jnp.where((lane128 // GLA_DK) == hd, qk, 0.0), axis=-1, keepdims=True)
            o = o + jnp.where((lane256 // GLA_DV) == hd, dot_h * v, 0.0)
        rows.append(o)
        ecol = jnp.sum(jnp.where(eye, jnp.broadcast_to(ea, (128, 128)), 0.0), axis=1, keepdims=True)
        kcol = jnp.sum(jnp.where(eye, jnp.broadcast_to(k, (128, 128)), 0.0), axis=1, keepdims=True)
        s_new = s_bd * ecol + jnp.where(bd, kcol * v, 0.0)
        for hd in range(GLA_HEADS):
            st_ref[j, hd] = s_new[hd * GLA_DK:(hd + 1) * GLA_DK, hd * GLA_DV:(hd + 1) * GLA_DV]
    o_all = jnp.concatenate(rows, axis=0)
    o_ref[...] = _gla_out_norm(o_all, g_ref[:, 640:896], onorm_ref[...]).astype(BF16)


def _gla_step_call(gla_in, state, onorm):
    t = gla_in.shape[0]
    nb = 8
    return pl.pallas_call(
        functools.partial(_gla_step_kernel, nb=nb),
        grid=(t // nb,),
        in_specs=[pl.BlockSpec((nb, GLA_PACK), lambda i: (i, 0)),
                  pl.BlockSpec((nb, GLA_HEADS, GLA_DK, GLA_DV), lambda i: (i, 0, 0, 0)),
                  pl.BlockSpec((1, 256), lambda i: (0, 0))],
        out_specs=[pl.BlockSpec((nb, 256), lambda i: (i, 0)),
                   pl.BlockSpec((nb, GLA_HEADS, GLA_DK, GLA_DV), lambda i: (i, 0, 0, 0))],
        out_shape=[jax.ShapeDtypeStruct((t, 256), BF16),
                   jax.ShapeDtypeStruct((t, GLA_HEADS, GLA_DK, GLA_DV), F32)],
        compiler_params=_cparams(1), name="gla_step",
    )(gla_in, state, onorm)


def _post_kernel(og_ref, oa_ref, wout_ref, x_ref, gm_ref, gffn_ref, sh_ref, sc_ref, rw_ref, rb_ref,
                 xmid_ref, t3_ref, eidx_ref, ew_ref, rank_ref, cnt_ref, carry_ref, *, per_token):
    i = pl.program_id(0)
    tm = x_ref.shape[0]
    gm, sh, sc = (gm_ref[...], sh_ref[...], sc_ref[...]) if per_token else (gm_ref[0], sh_ref[0], sc_ref[0])
    mix = _dot(og_ref[...], wout_ref[0:256, :]) + _dot(oa_ref[...], wout_ref[256:1024, :])
    xm = x_ref[...] + gm * mix
    xmid_ref[...] = xm
    t = _rms(xm, gffn_ref[...]) * (1.0 + sc) + sh
    t3_ref[:, 0, :] = _pack_rows(t)

    t_hi, t_lo = _split(t, 2)
    logits = _dot(t_hi, rw_ref[0]) + _dot(t_hi, rw_ref[1]) + _dot(t_lo, rw_ref[0])
    lt = jnp.transpose(logits)[0:N_EXPERTS, :]
    scores = 1.0 / (1.0 + jnp.exp(-lt))
    biased = scores + rb_ref[...]
    sc_rows = [scores[e:e + 1, :] for e in range(N_EXPERTS)]
    b_rows = [biased[e:e + 1, :] for e in range(N_EXPERTS)]

    def first_max(vals):
        m = vals[0]
        for a in vals[1:]:
            m = jnp.maximum(m, a)
        idx = jnp.full(m.shape, len(vals) - 1, I32)
        for j in range(len(vals) - 2, -1, -1):
            idx = jnp.where(vals[j] == m, j, idx)
        return m, idx

    gsum, loc1, loc2 = [], [], []
    for g in range(N_GROUPS):
        a = b_rows[g * EXPERTS_PER_GROUP:(g + 1) * EXPERTS_PER_GROUP]
        m1, i1 = first_max(a)
        a2 = [jnp.where(i1 == j, -jnp.inf, a[j]) for j in range(EXPERTS_PER_GROUP)]
        m2, i2 = first_max(a2)
        gsum.append(m1 + m2)
        loc1.append(i1)
        loc2.append(i2)
    _, gsel = first_max(gsum)
    l1, l2 = loc1[N_GROUPS - 1], loc2[N_GROUPS - 1]
    for g in range(N_GROUPS - 2, -1, -1):
        l1 = jnp.where(gsel == g, loc1[g], l1)
        l2 = jnp.where(gsel == g, loc2[g], l2)
    e1 = gsel * EXPERTS_PER_GROUP + l1
    e2 = gsel * EXPERTS_PER_GROUP + l2
    w1 = jnp.zeros_like(sc_rows[0])
    w2 = jnp.zeros_like(sc_rows[0])
    for e in range(N_EXPERTS):
        w1 = jnp.where(e1 == e, sc_rows[e], w1)
        w2 = jnp.where(e2 == e, sc_rows[e], w2)
    den = w1 + w2
    eidx_ref[0:1, :] = e1
    eidx_ref[1:2, :] = e2
    ew_ref[0:1, :] = w1 / den
    ew_ref[1:2, :] = w2 / den

    @pl.when(i == 0)
    def _():
        carry_ref[...] = jnp.zeros_like(carry_ref)
    sub = lax.broadcasted_iota(I32, (N_EXPERTS, tm), 0)
    rr = lax.broadcasted_iota(I32, (tm, tm), 0)
    cc = lax.broadcasted_iota(I32, (tm, tm), 1)
    triu = jnp.where(rr <= cc, 1.0, 0.0).astype(BF16)
    base = carry_ref[:, 0:1]
    for kk, ek in enumerate((e1, e2)):
        oh = jnp.where(sub == ek, 1.0, 0.0)
        cum = _dot(oh.astype(BF16), triu)
        rank = jnp.sum(oh * (base + cum - oh), axis=0, keepdims=True)
        rank_ref[kk:kk + 1, :] = rank.astype(I32)
        base = base + cum[:, tm - 1:tm]
    carry_ref[...] = jnp.broadcast_to(base, carry_ref.shape)
    cnt_ref[...] = jnp.broadcast_to(base, cnt_ref.shape)


def _post_call(o_gla, o_attn, w_out, x2d, gate, gffn, shift, scale, rw_pieces, rb_col, *, tm, tiles_per_seq, per_token):
    t = x2d.shape[0]
    nt = t // tm
    full = lambda a: pl.BlockSpec(a.shape, lambda i: (0,) * a.ndim)
    tok = lambda w: pl.BlockSpec((tm, w), lambda i: (i, 0))
    mod_spec = tok(D_MODEL) if per_token else pl.BlockSpec((1, 1, D_MODEL), lambda i: (i // tiles_per_seq, 0, 0))
    rowblk = pl.BlockSpec((2, tm), lambda i: (0, i))
    return pl.pallas_call(
        functools.partial(_post_kernel, per_token=per_token),
        grid=(nt,),
        in_specs=[tok(256), tok(768), full(w_out), tok(D_MODEL), mod_spec, full(gffn), mod_spec, mod_spec,
                  full(rw_pieces), full(rb_col)],
        out_specs=[tok(D_MODEL), pl.BlockSpec((tm, 1, 512), lambda i: (i, 0, 0)), rowblk, rowblk, rowblk,
                   pl.BlockSpec((N_EXPERTS, LANES), lambda i: (0, 0))],
        out_shape=[jax.ShapeDtypeStruct((t, D_MODEL), F32), jax.ShapeDtypeStruct((t, 1, 512), U32),
                   jax.ShapeDtypeStruct((2, t), I32), jax.ShapeDtypeStruct((2, t), F32),
                   jax.ShapeDtypeStruct((2, t), I32), jax.ShapeDtypeStruct((N_EXPERTS, LANES), F32)],
        scratch_shapes=[pltpu.VMEM((N_EXPERTS, LANES), F32)],
        compiler_params=_cparams(1), name="post_mix_route",
    )(o_gla, o_attn, w_out, x2d, gate, gffn, shift, scale, rw_pieces, rb_col)


def _row_copy(src, s_row, dst, d_row, sem):
    return pltpu.make_async_copy(src.at[pl.ds(s_row, 1)], dst.at[pl.ds(d_row, 1)], sem)


def _scatter_kernel(ps_ref, eidx_ref, rank_ref, t3_ref, xs_in_ref, xs_ref, sem, *, tm):
    del xs_in_ref
    i = pl.program_id(0)
    n = pl.num_programs(0)
    base = i * tm

    def issue(r, c):
        for kk in range(2):
            pos = ps_ref[eidx_ref[kk, r]] + rank_ref[kk, r]
            _row_copy(t3_ref, base + r, xs_ref, pos, sem).start()
        return c

    def drain(r, c):
        _row_copy(t3_ref, 0, xs_ref, 0, sem).wait()
        return c

    lax.fori_loop(0, tm, issue, 0)

    @pl.when(i > 0)
    def _():
        lax.fori_loop(0, 2 * tm, drain, 0)

    @pl.when(i == n - 1)
    def _():
        lax.fori_loop(0, 2 * tm, drain, 0)


def _scatter_call(pstart, eidx, rank, t3, n_slots, *, tm):
    t = t3.shape[0]
    zeros = jnp.zeros((n_slots, 1, 512), U32)
    grid_spec = pltpu.PrefetchScalarGridSpec(
        num_scalar_prefetch=1, grid=(t // tm,),
        in_specs=[pl.BlockSpec((2, tm), lambda i, ps: (0, i), memory_space=pltpu.SMEM),
                  pl.BlockSpec((2, tm), lambda i, ps: (0, i), memory_space=pltpu.SMEM),
                  pl.BlockSpec(memory_space=pl.ANY), pl.BlockSpec(memory_space=pl.ANY)],
        out_specs=pl.BlockSpec(memory_space=pl.ANY),
        scratch_shapes=[pltpu.SemaphoreType.DMA(())])
    return pl.pallas_call(
        functools.partial(_scatter_kernel, tm=tm), grid_spec=grid_spec,
        out_shape=jax.ShapeDtypeStruct((n_slots, 1, 512), U32),
        input_output_aliases={4: 0},
        compiler_params=_cparams(1), name="moe_dispatch",
    )(pstart, eidx, rank, t3, zeros)


def _expert_kernel(be_ref, nu_ref, xs_ref, w1_ref, w3_ref, w2_ref, ys_ref):
    j = pl.program_id(0)

    @pl.when(j < nu_ref[0])
    def _():
        xa, xb = _unpack_rows(xs_ref[:, 0, :])
        xa, xb = xa.astype(BF16), xb.astype(BF16)
        h1 = _dot(xa, w1_ref[0, 0:512, :]) + _dot(xb, w1_ref[0, 512:1024, :])
        h3 = _dot(xa, w3_ref[0, 0:512, :]) + _dot(xb, w3_ref[0, 512:1024, :])
        y = _dot((_silu(h1) * h3).astype(BF16), w2_ref[0])
        ys_ref[:, 0, :] = _pack_rows(y)

    @pl.when(j >= nu_ref[0])
    def _():
        ys_ref[...] = jnp.zeros_like(ys_ref)


def _expert_call(block_expert, n_used, xs, w1, w3, w2):
    n_slots = xs.shape[0]
    nb = n_slots // MOE_BLK
    grid_spec = pltpu.PrefetchScalarGridSpec(
        num_scalar_prefetch=2, grid=(nb,),
        in_specs=[pl.BlockSpec((MOE_BLK, 1, 512), lambda j, be, nu: (j, 0, 0)),
                  pl.BlockSpec((1, D_MODEL, D_EXPERT), lambda j, be, nu: (be[j], 0, 0)),
                  pl.BlockSpec((1, D_MODEL, D_EXPERT), lambda j, be, nu: (be[j], 0, 0)),
                  pl.BlockSpec((1, D_EXPERT, D_MODEL), lambda j, be, nu: (be[j], 0, 0))],
        out_specs=pl.BlockSpec((MOE_BLK, 1, 512), lambda j, be, nu: (j, 0, 0)))
    return pl.pallas_call(
        _expert_kernel, grid_spec=grid_spec,
        out_shape=jax.ShapeDtypeStruct((n_slots, 1, 512), U32),
        compiler_params=_cparams(1), name="moe_experts",
    )(block_expert, n_used, xs, w1, w3, w2)


def _combine_kernel(ps_ref, eidx_ref, rank_ref, ys_ref, xmid_ref, gc_ref, wcol_ref, o_ref, buf, sem, *, tm, per_token):
    def issue(r, c):
        for kk in range(2):
            pos = ps_ref[eidx_ref[kk, r]] + rank_ref[kk, r]
            _row_copy(ys_ref, pos, buf.at[kk], r, sem).start()
        return c

    def drain(r, c):
        _row_copy(ys_ref, 0, buf.at[0], 0, sem).wait()
        return c

    lax.fori_loop(0, tm, issue, 0)
    lax.fori_loop(0, 2 * tm, drain, 0)
    gc = gc_ref[...] if per_token else gc_ref[0]
    w = wcol_ref[...]
    a0, b0 = _unpack_rows(buf[0, :, 0, :])
    a1, b1 = _unpack_rows(buf[1, :, 0, :])
    w0, w1 = w[:, 0:1], w[:, 1:2]
    o_ref[:, 0:512] = xmid_ref[:, 0:512] + gc[:, 0:512] * (w0 * a0 + w1 * a1)
    o_ref[:, 512:1024] = xmid_ref[:, 512:1024] + gc[:, 512:1024] * (w0 * b0 + w1 * b1)


def _combine_call(pstart, eidx, rank, ys, xmid, gate, wcol, *, tm, tiles_per_seq, per_token):
    t = xmid.shape[0]
    mod_spec = (pl.BlockSpec((tm, D_MODEL), lambda i, ps: (i, 0)) if per_token
                else pl.BlockSpec((1, 1, D_MODEL), lambda i, ps: (i // tiles_per_seq, 0, 0)))
    grid_spec = pltpu.PrefetchScalarGridSpec(
        num_scalar_prefetch=1, grid=(t // tm,),
        in_specs=[pl.BlockSpec((2, tm), lambda i, ps: (0, i), memory_space=pltpu.SMEM),
                  pl.BlockSpec((2, tm), lambda i, ps: (0, i), memory_space=pltpu.SMEM),
                  pl.BlockSpec(memory_space=pl.ANY),
                  pl.BlockSpec((tm, D_MODEL), lambda i, ps: (i, 0)), mod_spec,
                  pl.BlockSpec((tm, 2), lambda i, ps: (i, 0))],
        out_specs=pl.BlockSpec((tm, D_MODEL), lambda i, ps: (i, 0)),
        scratch_shapes=[pltpu.VMEM((2, tm, 1, 512), U32), pltpu.SemaphoreType.DMA(())])
    return pl.pallas_call(
        functools.partial(_combine_kernel, tm=tm, per_token=per_token), grid_spec=grid_spec,
        out_shape=jax.ShapeDtypeStruct((t, D_MODEL), F32),
        compiler_params=_cparams(1), name="moe_combine",
    )(pstart, eidx, rank, ys, xmid, gate, wcol)


def _moe(t3, eidx, ew, rank, counts, xmid, gate, w1, w3, w2, *, tm, tiles_per_seq, per_token):
    t = xmid.shape[0]
    cnt = counts[:, 0].astype(I32)
    padded = (cnt + MOE_BLK - 1) // MOE_BLK * MOE_BLK
    pend = jnp.cumsum(padded)
    pstart = (pend - padded).astype(I32)
    nb = -(-2 * t // MOE_BLK) + N_EXPERTS
    block_expert = jnp.clip(jnp.searchsorted(pend, jnp.arange(nb, dtype=I32) * MOE_BLK, side="right"),
                            0, N_EXPERTS - 1).astype(I32)
    n_used = (pend[-1:] // MOE_BLK).astype(I32)
    xs = _scatter_call(pstart, eidx, rank, t3, nb * MOE_BLK, tm=tm)
    ys = _expert_call(block_expert, n_used, xs, w1, w3, w2)
    return _combine_call(pstart, eidx, rank, ys, xmid, gate, jnp.transpose(ew), tm=tm,
                         tiles_per_seq=tiles_per_seq, per_token=per_token)


def _qprep_kernel(qall_ref, wukt_ref, ql_ref, qr_ref, qf_ref):
    for hd in range(MLA_HEADS):
        blk = qall_ref[:, hd * LANES:(hd + 1) * LANES]
        ql_ref[:, hd, :] = _dot(blk, wukt_ref[hd])
        qr_ref[:, hd, :] = blk[:, MLA_NOPE:MLA_QK].astype(F32)
    for hd in range(FOX_HEADS):
        blk = qall_ref[:, (MLA_HEADS + hd) * LANES:(MLA_HEADS + hd + 1) * LANES].astype(F32)
        if hd // FOX_GROUP == 1:
            blk = pltpu.roll(blk, FOX_DH, 1)
        qf_ref[:, hd, :] = blk


def _qprep_call(q_all, wukt_p):
    t = q_all.shape[0]
    return pl.pallas_call(
        _qprep_kernel,
        out_shape=[jax.ShapeDtypeStruct((t, MLA_HEADS, MLA_KV_RANK), F32),
                   jax.ShapeDtypeStruct((t, MLA_HEADS, MLA_ROPE), F32),
                   jax.ShapeDtypeStruct((t, FOX_HEADS, LANES), F32)],
        compiler_params=pltpu.CompilerParams(vmem_limit_bytes=VMEM_LIMIT), name="sample_qprep",
    )(q_all, wukt_p)


def _foxbias_kernel(pt_ref, lf_hbm, o_ref, buf, sem, *, layer, n_pages):
    b = pl.program_id(0)
    nb = pl.num_programs(0)

    def copies(bb, slot):
        return [pltpu.make_async_copy(lf_hbm.at[layer, pl.ds(pt_ref[bb, p], 1)], buf.at[slot, pl.ds(p, 1)], sem.at[slot])
                for p in range(n_pages)]

    @pl.when(b == 0)
    def _():
        for cp in copies(0, 0):
            cp.start()

    slot = b % 2

    @pl.when(b + 1 < nb)
    def _():
        for cp in copies(b + 1, 1 - slot):
            cp.start()

    for cp in copies(b, slot):
        cp.wait()

    lf = buf[slot, :, 0, :]
    rr = lax.broadcasted_iota(I32, (512, 512), 0)
    cc = lax.broadcasted_iota(I32, (512, 512), 1)
    same_head = (rr % FOX_HEADS) == (cc // PAGE_SIZE)
    m_tot = jnp.where(same_head, 1.0, 0.0).astype(BF16)
    m_suf = jnp.where(same_head & ((rr // FOX_HEADS) > (cc % PAGE_SIZE)), 1.0, 0.0).astype(BF16)
    pr = lax.broadcasted_iota(I32, (n_pages, n_pages), 0)
    pc = lax.broadcasted_iota(I32, (n_pages, n_pages), 1)
    later = jnp.where(pc > pr, 1.0, 0.0).astype(BF16)
    pieces = _split(lf, 3)
    within = sum(_dot(p, m_suf) for p in pieces)
    total = sum(_dot(p, m_tot) for p in pieces)
    res = within + _dot_split(later, total, 3)
    for hd in range(FOX_HEADS):
        o_ref[0, hd] = res[:, hd * PAGE_SIZE:(hd + 1) * PAGE_SIZE]


def _foxbias_call(page_table, lf_cache, layer):
    nb, n_pages = page_table.shape
    grid_spec = pltpu.PrefetchScalarGridSpec(
        num_scalar_prefetch=1, grid=(nb,),
        in_specs=[pl.BlockSpec(memory_space=pl.ANY)],
        out_specs=pl.BlockSpec((1, FOX_HEADS, n_pages, PAGE_SIZE), lambda b, pt: (b, 0, 0, 0)),
        scratch_shapes=[pltpu.VMEM((2, n_pages, 1, 512), F32), pltpu.SemaphoreType.DMA((2,))])
    return pl.pallas_call(
        functools.partial(_foxbias_kernel, layer=layer, n_pages=n_pages), grid_spec=grid_spec,
        out_shape=jax.ShapeDtypeStruct((nb, FOX_HEADS, n_pages, PAGE_SIZE), F32),
        compiler_params=_cparams(1), name="fox_past_bias",
    )(page_table, lf_cache)


def _paged_kernel(pt_ref, ql_ref, qr_ref, qf_ref, bias_ref, gnew_ref, ckvn_ref, krn_ref, fkn_ref, fvn_ref,
                  ckv_hbm, kr_hbm, fk_hbm, fv_hbm, ol_ref, of_ref,
                  ckv_buf, kr_buf, fk_buf, fv_buf, sem, m_ref, l_ref, acc_ref, mf_ref, lf_ref, accf_ref,
                  *, layer, n_chunks, pg):
    s = pl.program_id(0)
    ns = pl.num_programs(0)
    chunk = s % n_chunks

    def copies(step, slot):
        bb, ch = step // n_chunks, step % n_chunks
        out = []
        for p in range(pg):
            phys = pt_ref[bb, ch * pg + p]
            rows = pl.ds(p * PAGE_SIZE, PAGE_SIZE)
            out.append(pltpu.make_async_copy(ckv_hbm.at[layer, phys], ckv_buf.at[slot, rows], sem.at[slot, 0]))
            out.append(pltpu.make_async_copy(kr_hbm.at[layer, phys], kr_buf.at[slot, rows], sem.at[slot, 1]))
            out.append(pltpu.make_async_copy(fk_hbm.at[layer, phys], fk_buf.at[slot, rows], sem.at[slot, 2]))
            out.append(pltpu.make_async_copy(fv_hbm.at[layer, phys], fv_buf.at[slot, rows], sem.at[slot, 3]))
        return out

    @pl.when(s == 0)
    def _():
        for cp in copies(0, 0):
            cp.start()

    slot = s % 2

    @pl.when(s + 1 < ns)
    def _():
        for cp in copies(s + 1, 1 - slot):
            cp.start()

    @pl.when(chunk == 0)
    def _():
        m_ref[...] = jnp.full_like(m_ref, -1e30)
        l_ref[...] = jnp.zeros_like(l_ref)
        acc_ref[...] = jnp.zeros_like(acc_ref)
        mf_ref[...] = jnp.full_like(mf_ref, -1e30)
        lf_ref[...] = jnp.zeros_like(lf_ref)
        accf_ref[...] = jnp.zeros_like(accf_ref)

    for cp in copies(s, slot):
        cp.wait()

    def merge(m_r, l_r, acc_r, sc, pv):
        m_old = m_r[:, 0:1]
        m_new = jnp.maximum(m_old, jnp.max(sc, axis=-1, keepdims=True))
        alpha = jnp.exp(m_old - m_new)
        p = jnp.exp(sc - m_new)
        l_r[...] = jnp.broadcast_to(alpha * l_r[:, 0:1] + jnp.sum(p, axis=-1, keepdims=True), l_r.shape)
        acc_r[...] = alpha * acc_r[...] + pv(p)
        m_r[...] = jnp.broadcast_to(m_new, m_r.shape)

    ql, qr, qf = ql_ref[0], qr_ref[0], qf_ref[0]
    ckv = ckv_buf[slot]
    merge(m_ref, l_ref, acc_ref, _dot_nt(ql, ckv) + _dot_nt(qr, kr_buf[slot]), lambda p: _dot(p, ckv))
    fv = fv_buf[slot]
    merge(mf_ref, lf_ref, accf_ref, _dot_nt(qf, fk_buf[slot]) + bias_ref[0] + gnew_ref[0], lambda p: _dot(p, fv))

    @pl.when(chunk == n_chunks - 1)
    def _():
        ckvn = ckvn_ref[0]
        s_new = jnp.sum(ql * ckvn, axis=-1, keepdims=True) + jnp.sum(qr * krn_ref[0], axis=-1, keepdims=True)
        merge(m_ref, l_ref, acc_ref, s_new, lambda p: p * ckvn)
        ol_ref[0] = acc_ref[...] / l_ref[:, 0:1]
        fvn = fvn_ref[0]
        sf_new = jnp.sum(qf * fkn_ref[0], axis=-1, keepdims=True)
        merge(mf_ref, lf_ref, accf_ref, sf_new, lambda p: p * fvn)
        of_ref[0] = accf_ref[...] / lf_ref[:, 0:1]


def _paged_call(page_table, ql, qr, qf, bias, gnew, ckvn, krn, fkn, fvn, c_ckv, c_kr, c_fk, c_fv, layer, *, pg):
    nb, n_pages = page_table.shape
    n_chunks = n_pages // pg
    kc = pg * PAGE_SIZE
    per_b = lambda shp: pl.BlockSpec((1,) + shp, lambda s, pt: (s // n_chunks,) + (0,) * len(shp))
    grid_spec = pltpu.PrefetchScalarGridSpec(
        num_scalar_prefetch=1, grid=(nb * n_chunks,),
        in_specs=[per_b((MLA_HEADS, MLA_KV_RANK)), per_b((MLA_HEADS, MLA_ROPE)), per_b((FOX_HEADS, LANES)),
                  pl.BlockSpec((1, FOX_HEADS, kc), lambda s, pt: (s // n_chunks, 0, s % n_chunks)),
                  per_b((FOX_HEADS, 1)), per_b((1, MLA_KV_RANK)), per_b((1, MLA_ROPE)), per_b((1, LANES)),
                  per_b((1, LANES))] + [pl.BlockSpec(memory_space=pl.ANY)] * 4,
        out_specs=[per_b((MLA_HEADS, MLA_KV_RANK)), per_b((FOX_HEADS, LANES))],
        scratch_shapes=[pltpu.VMEM((2, kc, MLA_KV_RANK), F32), pltpu.VMEM((2, kc, MLA_ROPE), F32),
                        pltpu.VMEM((2, kc, LANES), F32), pltpu.VMEM((2, kc, LANES), F32),
                        pltpu.SemaphoreType.DMA((2, 4)),
                        pltpu.VMEM((MLA_HEADS, LANES), F32), pltpu.VMEM((MLA_HEADS, LANES), F32),
                        pltpu.VMEM((MLA_HEADS, MLA_KV_RANK), F32),
                        pltpu.VMEM((FOX_HEADS, LANES), F32), pltpu.VMEM((FOX_HEADS, LANES), F32),
                        pltpu.VMEM((FOX_HEADS, LANES), F32)])
    return pl.pallas_call(
        functools.partial(_paged_kernel, layer=layer, n_chunks=n_chunks, pg=pg), grid_spec=grid_spec,
        out_shape=[jax.ShapeDtypeStruct((nb, MLA_HEADS, MLA_KV_RANK), F32),
                   jax.ShapeDtypeStruct((nb, FOX_HEADS, LANES), F32)],
        compiler_params=_cparams(1), name="paged_attn",
    )(page_table, ql, qr, qf, bias, gnew, ckvn, krn, fkn, fvn, c_ckv, c_kr, c_fk, c_fv)


def _oproj_kernel(ol_ref, of_ref, wuv_ref, o_ref):
    for hd in range(MLA_HEADS):
        o_ref[:, hd * MLA_V:(hd + 1) * MLA_V] = _dot(ol_ref[:, hd, :].astype(BF16), wuv_ref[hd]).astype(BF16)
    for hd in range(FOX_HEADS):
        n = hd // FOX_GROUP
        o_ref[:, 512 + hd * FOX_DH:512 + (hd + 1) * FOX_DH] = of_ref[:, hd, n * FOX_DH:(n + 1) * FOX_DH].astype(BF16)


def _oproj_call(o_lat, o_fox, wuv_h):
    t = o_lat.shape[0]
    return pl.pallas_call(
        _oproj_kernel, out_shape=jax.ShapeDtypeStruct((t, ATT_HEADS * 64), BF16),
        compiler_params=pltpu.CompilerParams(vmem_limit_bytes=VMEM_LIMIT), name="sample_oproj",
    )(o_lat, o_fox, wuv_h)


def kernel(x_prompt, x_sample, cache_mla_ckv, cache_mla_krope, cache_fox_k, cache_fox_v, cache_fox_logf, state_gla, page_table, c_prompt, c_sample, router_w, router_b, attn_norm, ffn_norm, ada_w, ada_b, w_in, gla_wa2, gla_ba, gla_onorm, mla_qlat_norm, mla_wuq, mla_q_norm, mla_kv_norm, mla_kr_norm, mla_wuk, mla_wuv, fox_q_norm, fox_k_norm, fox_fb, w_out, moe_w1, moe_w3, moe_w2):
    bp, seq, _ = x_prompt.shape
    bs = x_sample.shape[0]
    depth = w_in.shape[0]
    n_pages = page_table.shape[1]
    tp = bp * seq
    tm_p = min(TM_PROMPT, seq)
    tps = seq // tm_p
    tq = min(TQ, seq)
    sup = min(GLA_SUPER, seq)
    pg = min(PAGES_PER_STEP, n_pages)

    mods = _ada_call(jnp.concatenate([c_prompt, c_sample], axis=0), ada_w, ada_b)
    cos_p, sin_p = _rope_tables(jnp.arange(seq))
    cos_s, sin_s = _rope_tables(jnp.full((1,), n_pages * PAGE_SIZE))
    aug = _aug_constants()
    rw_pad = jnp.zeros((D_MODEL, LANES), F32).at[:, :N_EXPERTS].set(router_w)
    rw_hi = rw_pad.astype(BF16)
    rw_pieces = jnp.stack([rw_hi, (rw_pad - rw_hi.astype(F32)).astype(BF16)])
    rb_col = router_b.astype(F32)[:, None]
    n_pool = cache_mla_ckv.shape[1]
    c_fk = cache_fox_k.reshape(depth, n_pool, PAGE_SIZE, LANES)
    c_fv = cache_fox_v.reshape(depth, n_pool, PAGE_SIZE, LANES)
    c_lf = cache_fox_logf.reshape(depth, n_pool, 1, PAGE_SIZE * FOX_HEADS)

    xp = x_prompt.reshape(tp, D_MODEL)
    xs = x_sample.reshape(bs, D_MODEL)
    p_ent, s_ent = [], []
    for layer in range(depth):
        lw = _layer_weights(layer, w_in, gla_wa2, gla_ba, gla_onorm, mla_qlat_norm, mla_wuq, mla_q_norm,
                            mla_kv_norm, mla_kr_norm, mla_wuk, mla_wuv, fox_q_norm, fox_k_norm, fox_fb, attn_norm)
        wout_b = w_out[layer].astype(BF16)
        w1b, w3b, w2b = moe_w1[layer].astype(BF16), moe_w3[layer].astype(BF16), moe_w2[layer].astype(BF16)
        gffn = ffn_norm[layer][None, :]
        mod = mods[layer]
        mp = [mod[:bp, j * D_MODEL:(j + 1) * D_MODEL].reshape(bp, 1, D_MODEL) for j in range(6)]
        ms = [mod[bp:, j * D_MODEL:(j + 1) * D_MODEL] for j in range(6)]

        (q_all, k_all, v_all, ckv, kr, fk, fv, lf, gla_in) = _premix_call(
            xp, mp[0], mp[1], cos_p, sin_p, lw, aug, tm=tm_p, tiles_per_seq=tps, per_token=False)
        o_gla, st = _gla_call(gla_in.reshape(bp, seq, GLA_PACK), lw["onorm"], sup=sup)
        o_att = _flash_call(q_all.reshape(bp, seq, -1), k_all.reshape(bp, seq, -1), v_all.reshape(bp, seq, -1), tq=tq)
        p_ent.append((ckv.reshape(bp, seq, MLA_KV_RANK), kr.reshape(bp, seq, MLA_ROPE),
                      fk.reshape(bp, seq, FOX_KV_HEADS, FOX_DH), fv.reshape(bp, seq, FOX_KV_HEADS, FOX_DH),
                      lf.reshape(bp, seq, FOX_HEADS), st))
        xmid, t3, eidx, ew, rank, counts = _post_call(
            o_gla.reshape(tp, 256), o_att.reshape(tp, -1), wout_b, xp, mp[2], gffn, mp[3], mp[4], rw_pieces, rb_col,
            tm=tm_p, tiles_per_seq=tps, per_token=False)
        xp = _moe(t3, eidx, ew, rank, counts, xmid, mp[5], w1b, w3b, w2b, tm=tm_p, tiles_per_seq=tps, per_token=False)

        (q_all, _, _, ckv, kr, fk, fv, lf, gla_in) = _premix_call(
            xs, ms[0], ms[1], cos_s, sin_s, lw, aug, tm=bs, tiles_per_seq=1, per_token=True)
        o_gla, st = _gla_step_call(gla_in, state_gla[layer], lw["onorm"])
        wukt_p = jnp.concatenate([jnp.transpose(lw["wuk_raw"], (1, 2, 0)),
                                  jnp.zeros((MLA_HEADS, LANES - MLA_NOPE, MLA_KV_RANK), F32)], axis=1).astype(BF16)
        ql, qr, qf = _qprep_call(q_all, wukt_p)
        bias = _foxbias_call(page_table, c_lf, layer).reshape(bs, FOX_HEADS, n_pages * PAGE_SIZE)
        o_lat, o_fox = _paged_call(page_table, ql, qr, qf, bias, lf.reshape(bs, FOX_HEADS, 1),
                                   ckv.reshape(bs, 1, -1), kr.reshape(bs, 1, -1), fk.reshape(bs, 1, -1),
                                   fv.reshape(bs, 1, -1), cache_mla_ckv, cache_mla_krope, c_fk, c_fv, layer, pg=pg)
        o_att = _oproj_call(o_lat, o_fox, jnp.transpose(lw["wuv_raw"], (1, 0, 2)).astype(BF16))
        s_ent.append((ckv.reshape(bs, 1, MLA_KV_RANK), kr.reshape(bs, 1, MLA_ROPE),
                      fk.reshape(bs, 1, FOX_KV_HEADS, FOX_DH), fv.reshape(bs, 1, FOX_KV_HEADS, FOX_DH),
                      lf.reshape(bs, 1, FOX_HEADS), st))
        xmid, t3, eidx, ew, rank, counts = _post_call(
            o_gla, o_att, wout_b, xs, ms[2], gffn, ms[3], ms[4], rw_pieces, rb_col,
            tm=bs, tiles_per_seq=1, per_token=True)
        xs = _moe(t3, eidx, ew, rank, counts, xmid, ms[5], w1b, w3b, w2b, tm=bs, tiles_per_seq=1, per_token=True)

    stack = lambda ents, j: jnp.stack([e[j] for e in ents], axis=0)
    return (xp.reshape(bp, seq, D_MODEL), xs.reshape(bs, 1, D_MODEL),
            stack(p_ent, 0), stack(p_ent, 1), stack(p_ent, 2), stack(p_ent, 3), stack(p_ent, 4), stack(p_ent, 5),
            stack(s_ent, 0), stack(s_ent, 1), stack(s_ent, 2), stack(s_ent, 3), stack(s_ent, 4), stack(s_ent, 5))
```

```python
import functools

import numpy as np
import jax
import jax.numpy as jnp
from jax import lax
from jax.experimental import pallas as pl
from jax.experimental.pallas import tpu as pltpu

F32, BF16, I32, U32 = jnp.float32, jnp.bfloat16, jnp.int32, jnp.uint32

D_MODEL = 1024
EPS = 1e-6
GLA_HEADS, GLA_DK, GLA_DV, GLA_RANK, GLA_TAU, GLA_CHUNK = 4, 32, 64, 16, 16.0, 64
MLA_HEADS, MLA_Q_RANK, MLA_KV_RANK, MLA_NOPE, MLA_ROPE, MLA_V = 8, 384, 256, 64, 32, 64
MLA_QK = MLA_NOPE + MLA_ROPE
MLA_SCALE = MLA_QK ** -0.5
ROPE_THETA = 10000.0
FOX_HEADS, FOX_KV_HEADS, FOX_GROUP, FOX_DH = 4, 2, 2, 64
FOX_SCALE = FOX_DH ** -0.5
N_EXPERTS, N_GROUPS, EXPERTS_PER_GROUP, D_EXPERT = 16, 4, 4, 512
PAGE_SIZE = 128
IN_SIZES = (128, 128, 256, 16, 256, 384, 256, 32, 256, 128, 128, 4)
ATT_HEADS = MLA_HEADS + FOX_HEADS
LOG2E = 1.4426950408889634

LANES = 128
VMEM_LIMIT = 56 * 1024 * 1024

O_GQ, O_GK, O_GV, O_GR, O_CQ, O_CKV, O_FQ, O_FK, O_FV, O_SM = 0, 128, 256, 512, 768, 1152, 1408, 1920, 2176, 2432
W_IN_PACKED = 2560
SM_GG, SM_FF, SM_KR = 0, 16, 64
GLA_PACK = 896

TM_PROMPT = 512
TM_MOE = 256
TQ = 512
GLA_SUPER = 512
MOE_BLK = 256
PAGES_PER_STEP = 32


def _cparams(n_axes):
    return pltpu.CompilerParams(dimension_semantics=("arbitrary",) * n_axes, vmem_limit_bytes=VMEM_LIMIT)


def _dot(a, b):
    return jnp.dot(a, b, preferred_element_type=F32)


def _dot_nt(a, b):
    return lax.dot_general(a, b, (((1,), (1,)), ((), ())), preferred_element_type=F32)


def _dot_tn(a, b):
    return lax.dot_general(a, b, (((0,), (0,)), ((), ())), preferred_element_type=F32)


def _split(x, n):
    out, r = [], x
    for _ in range(n):
        p = r.astype(BF16)
        out.append(p)
        r = r - p.astype(F32)
    return out


def _dot_split(a_exact, b, n):
    acc = None
    for p in _split(b, n):
        t = _dot(a_exact, p)
        acc = t if acc is None else acc + t
    return acc


def _logsig(x):
    return jnp.minimum(x, 0.0) - jnp.log1p(jnp.exp(-jnp.abs(x)))


def _silu(x):
    return x / (1.0 + jnp.exp(-x))


def _rms(x, gain):
    return x * lax.rsqrt(jnp.mean(x * x, axis=-1, keepdims=True) + EPS) * gain


def _pack_rows(x):
    hi = pltpu.bitcast(x[:, :512].astype(BF16).astype(F32), U32)
    lo = pltpu.bitcast(x[:, 512:].astype(BF16).astype(F32), U32) >> 16
    return hi | lo


def _unpack_rows(w):
    return pltpu.bitcast(w & jnp.uint32(0xFFFF0000), F32), pltpu.bitcast(w << 16, F32)


def _ada_kernel(c_ref, w_ref, b_ref, o_ref):
    s_hi, s_lo = _split(_silu(c_ref[...]), 2)
    w_hi, w_lo = _split(w_ref[0], 2)
    o_ref[0] = _dot(s_hi, w_hi) + _dot(s_hi, w_lo) + _dot(s_lo, w_hi) + b_ref[0]


def _ada_call(c_all, ada_w, ada_b):
    rows, depth, n = c_all.shape[0], ada_w.shape[0], ada_w.shape[2]
    tn = n // 4
    return pl.pallas_call(
        _ada_kernel, grid=(depth, n // tn),
        in_specs=[pl.BlockSpec((rows, D_MODEL), lambda l, j: (0, 0)),
                  pl.BlockSpec((1, D_MODEL, tn), lambda l, j: (l, 0, j)),
                  pl.BlockSpec((1, 1, tn), lambda l, j: (l, 0, j))],
        out_specs=pl.BlockSpec((1, rows, tn), lambda l, j: (l, 0, j)),
        out_shape=jax.ShapeDtypeStruct((depth, rows, n), F32),
        compiler_params=_cparams(2), name="ada_mod",
    )(c_all, ada_w, ada_b.reshape(depth, 1, n))


def _premix_kernel(x_ref, sh_ref, sc_ref, cos_ref, sin_ref, gat_ref, win_ref, gql_ref, wuq_ref, gqp_ref, gkv_ref,
                   wuk_ref, wuv_ref, gkr_ref, gfq_ref, gfk_ref, fb_ref, wa2_ref, ba_ref, epl_ref, oneq_ref, onek_ref,
                   qall_ref, kall_ref, vall_ref, ckv_ref, kr_ref, fk_ref, fv_ref, lf_ref, gla_ref, carry_ref,
                   *, tiles_per_seq, per_token):
    i = pl.program_id(0)
    tm = x_ref.shape[0]
    x = x_ref[...]
    sh, sc = (sh_ref[...], sc_ref[...]) if per_token else (sh_ref[0], sc_ref[0])
    h = _rms(x, gat_ref[...]) * (1.0 + sc) + sh
    z = _dot(h.astype(BF16), win_ref[...])

    lane = lax.broadcasted_iota(I32, (tm, LANES), 1)
    cos, sin = cos_ref[...], sin_ref[...]

    def rope(v):
        partner = jnp.where(lane < 80, pltpu.roll(v, 112, 1), pltpu.roll(v, 16, 1))
        return v * cos + partner * sin

    small = z[:, O_SM:O_SM + LANES]

    gate = _dot(small.astype(BF16), wa2_ref[...]) + ba_ref[...]
    gla_ref[:, 0:128] = z[:, O_GQ:O_GQ + 128] * (GLA_DK ** -0.5)
    gla_ref[:, 128:256] = z[:, O_GK:O_GK + 128]
    gla_ref[:, 256:384] = _logsig(gate) * (1.0 / GLA_TAU)
    gla_ref[:, 384:640] = z[:, O_GV:O_GV + 256]
    gla_ref[:, 640:896] = z[:, O_GR:O_GR + 256]

    ckvn = _rms(z[:, O_CKV:O_CKV + MLA_KV_RANK], gkv_ref[...])
    ckv_ref[...] = ckvn
    ckvb = ckvn.astype(BF16)
    knope = _dot(ckvb, wuk_ref[...])
    one64 = jnp.where(lane == MLA_V, 1.0, 0.0)
    vm = _dot(ckvb, wuv_ref[...])
    for hd in range(MLA_HEADS):
        vall_ref[:, hd * LANES:(hd + 1) * LANES] = (vm[:, hd * LANES:(hd + 1) * LANES] + one64).astype(BF16)
    krm = jnp.where((lane >= SM_KR) & (lane < SM_KR + MLA_ROPE), small, 0.0)
    krn = krm * lax.rsqrt(jnp.sum(krm * krm, axis=-1, keepdims=True) * (1.0 / MLA_ROPE) + EPS) * gkr_ref[...]
    kro = rope(krn)
    kr_ref[...] = kro[:, SM_KR:SM_KR + MLA_ROPE]

    q = _dot(_rms(z[:, O_CQ:O_CQ + MLA_Q_RANK], gql_ref[...]).astype(BF16), wuq_ref[...])
    for hd in range(MLA_HEADS):
        sl = slice(hd * LANES, (hd + 1) * LANES)
        qh = q[:, sl]
        ms = jnp.sum(qh * qh, axis=-1, keepdims=True) * (1.0 / MLA_QK)
        qh = rope(qh * lax.rsqrt(ms + EPS) * gqp_ref[...])
        qall_ref[:, sl] = (qh * (MLA_SCALE * LOG2E)).astype(BF16)
        kall_ref[:, sl] = (knope[:, sl] + kro).astype(BF16)

    kn = []
    for n in range(FOX_KV_HEADS):
        blk = z[:, O_FK + n * LANES:O_FK + (n + 1) * LANES]
        ms = jnp.sum(blk * blk, axis=-1, keepdims=True) * (1.0 / LANES)
        kn.append(blk * lax.rsqrt(ms + EPS) * gfk_ref[...])
    fk_ref[...] = jnp.where(lane < FOX_DH, kn[0], kn[1])
    fvd = z[:, O_FV:O_FV + 256]
    fv_ref[...] = jnp.where(lane < FOX_DH, fvd[:, 0:128], fvd[:, 128:256])
    for hd in range(FOX_HEADS):
        n = hd // FOX_GROUP
        vall_ref[:, (MLA_HEADS + hd) * LANES:(MLA_HEADS + hd + 1) * LANES] = jnp.where(
            lane < FOX_DH, fvd[:, n * LANES:(n + 1) * LANES], one64).astype(BF16)

    lfv = jnp.where((lane >= SM_FF) & (lane < SM_FF + FOX_HEADS), _logsig(small + fb_ref[...]), 0.0)
    lf_ref[...] = lfv[:, SM_FF:SM_FF + FOX_HEADS]

    if per_token:
        augq = augk = None
    else:
        @pl.when(i % tiles_per_seq == 0)
        def _():
            carry_ref[...] = jnp.zeros_like(carry_ref)
        row = lax.broadcasted_iota(I32, (tm, tm), 0)
        col = lax.broadcasted_iota(I32, (tm, tm), 1)
        tril = jnp.where(row >= col, 1.0, 0.0).astype(BF16)
        fcum = _dot_split(tril, lfv, 3) + carry_ref[0:1, :]
        carry_ref[0:1, :] = fcum[tm - 1:tm, :]
        aug = _dot(jnp.concatenate(_split(fcum * LOG2E, 3), axis=1), epl_ref[...])
        augq = aug[:, 0:512] + oneq_ref[...]
        augk = aug[:, 512:1024] + onek_ref[...]

    for hd in range(FOX_HEADS):
        blk = z[:, O_FQ + hd * LANES:O_FQ + (hd + 1) * LANES]
        ms = jnp.sum(blk * blk, axis=-1, keepdims=True) * (1.0 / FOX_DH)
        qh = blk * lax.rsqrt(ms + EPS) * gfq_ref[...] * (FOX_SCALE * LOG2E)
        kh = jnp.where(lane < FOX_DH, kn[hd // FOX_GROUP], 0.0)
        if augq is not None:
            qh = qh + augq[:, hd * LANES:(hd + 1) * LANES]
            kh = kh + augk[:, hd * LANES:(hd + 1) * LANES]
        sl = slice((MLA_HEADS + hd) * LANES, (MLA_HEADS + hd + 1) * LANES)
        qall_ref[:, sl] = qh.astype(BF16)
        kall_ref[:, sl] = kh.astype(BF16)


def _rope_tables(pos):
    half = MLA_ROPE // 2
    inv = ROPE_THETA ** (-jnp.arange(half, dtype=F32) / half)
    ang = pos.astype(F32)[:, None] * inv[None, :]
    c, s = jnp.cos(ang), jnp.sin(ang)
    n = pos.shape[0]
    cos_t = jnp.concatenate([jnp.ones((n, 64), F32), c, c, jnp.zeros((n, 32), F32)], axis=1)
    sin_t = jnp.concatenate([jnp.zeros((n, 64), F32), -s, s, jnp.zeros((n, 32), F32)], axis=1)
    return cos_t, sin_t


def _lane_vec(pieces):
    v = jnp.zeros((LANES,), F32)
    for off, val in pieces:
        v = lax.dynamic_update_slice(v, val.astype(F32), (off,))
    return v[None, :]


def _aug_constants():
    e = np.zeros((3 * LANES, 2 * FOX_HEADS * LANES), np.float32)
    oneq = np.zeros((1, FOX_HEADS * LANES), np.float32)
    onek = np.zeros((1, FOX_HEADS * LANES), np.float32)
    for hd in range(FOX_HEADS):
        for j in range(3):
            e[j * LANES + SM_FF + hd, hd * LANES + FOX_DH + j] = 1.0
            e[j * LANES + SM_FF + hd, FOX_HEADS * LANES + hd * LANES + FOX_DH + 3 + j] = -1.0
            oneq[0, hd * LANES + FOX_DH + 3 + j] = 1.0
            onek[0, hd * LANES + FOX_DH + j] = 1.0
    return jnp.asarray(e, BF16), jnp.asarray(oneq), jnp.asarray(onek)


def _layer_weights(layer, w_in, gla_wa2, gla_ba, gla_onorm, mla_qlat_norm, mla_wuq, mla_q_norm, mla_kv_norm,
                   mla_kr_norm, mla_wuk, mla_wuv, fox_q_norm, fox_k_norm, fox_fb, attn_norm):
    w = w_in[layer]
    offs = np.concatenate([[0], np.cumsum(IN_SIZES)])
    gq, gk, gv, gg, gr, cq, ckv, kr, fq, fk, fv, ff = [w[:, offs[j]:offs[j + 1]] for j in range(12)]
    z64 = jnp.zeros((D_MODEL, 64), F32)
    fq_p = jnp.concatenate([t for hd in range(FOX_HEADS) for t in (fq[:, hd * 64:(hd + 1) * 64], z64)], axis=1)
    fk_p = jnp.concatenate([fk[:, 0:64], fk[:, 0:64], fk[:, 64:128], fk[:, 64:128]], axis=1)
    fv_p = jnp.concatenate([fv[:, 0:64], fv[:, 0:64], fv[:, 64:128], fv[:, 64:128]], axis=1)
    sm = jnp.concatenate([gg, ff, jnp.zeros((D_MODEL, SM_KR - SM_FF - FOX_HEADS), F32), kr,
                          jnp.zeros((D_MODEL, LANES - SM_KR - MLA_ROPE), F32)], axis=1)
    win_p = jnp.concatenate([gq, gk, gv, gr, cq, ckv, fq_p, fk_p, fv_p, sm], axis=1).astype(BF16)

    wuq = mla_wuq[layer].reshape(MLA_Q_RANK, MLA_HEADS, MLA_QK)
    wuq_p = jnp.concatenate([wuq, jnp.zeros((MLA_Q_RANK, MLA_HEADS, LANES - MLA_QK), F32)], axis=2)
    wuq_p = wuq_p.reshape(MLA_Q_RANK, MLA_HEADS * LANES).astype(BF16)
    wuk = mla_wuk[layer]
    wuk_p = jnp.concatenate([wuk, jnp.zeros_like(wuk)], axis=2).reshape(MLA_KV_RANK, MLA_HEADS * LANES).astype(BF16)
    wuv = mla_wuv[layer]
    wuv_p = jnp.concatenate([wuv, jnp.zeros_like(wuv)], axis=2).reshape(MLA_KV_RANK, MLA_HEADS * LANES).astype(BF16)
    wa2_p = jnp.zeros((LANES, LANES), F32).at[SM_GG:SM_GG + GLA_RANK].set(gla_wa2[layer]).astype(BF16)
    return dict(
        win=win_p, wuq=wuq_p, wuk=wuk_p, wuv=wuv_p, wa2=wa2_p,
        gat=attn_norm[layer][None, :], gql=mla_qlat_norm[layer][None, :], gkv=mla_kv_norm[layer][None, :],
        gqp=_lane_vec([(0, mla_q_norm[layer])]),
        gkr=_lane_vec([(SM_KR, mla_kr_norm[layer])]),
        gfq=_lane_vec([(0, fox_q_norm[layer])]),
        gfk=_lane_vec([(0, fox_k_norm[layer]), (FOX_DH, fox_k_norm[layer])]),
        fb=_lane_vec([(SM_FF, fox_fb[layer])]),
        ba=gla_ba[layer][None, :],
        onorm=jnp.tile(gla_onorm[layer], GLA_HEADS)[None, :],
        wuk_raw=wuk, wuv_raw=mla_wuv[layer],
    )


def _premix_call(x2d, shift, scale, cos_t, sin_t, lw, aug, *, tm, tiles_per_seq, per_token):
    t = x2d.shape[0]
    nt = t // tm
    epl, oneq, onek = aug
    full = lambda a: pl.BlockSpec(a.shape, lambda i: (0,) * a.ndim)
    tok = lambda w: pl.BlockSpec((tm, w), lambda i: (i, 0))
    if per_token:
        mod_spec = tok(D_MODEL)
        pos_spec = full(cos_t)
    else:
        mod_spec = pl.BlockSpec((1, 1, D_MODEL), lambda i: (i // tiles_per_seq, 0, 0))
        pos_spec = pl.BlockSpec((tm, LANES), lambda i: (i % tiles_per_seq, 0))
    consts = [lw["gat"], lw["win"], lw["gql"], lw["wuq"], lw["gqp"], lw["gkv"], lw["wuk"], lw["wuv"], lw["gkr"],
              lw["gfq"], lw["gfk"], lw["fb"], lw["wa2"], lw["ba"], epl, oneq, onek]
    out_widths = [(ATT_HEADS * LANES, BF16), (ATT_HEADS * LANES, BF16), (ATT_HEADS * LANES, BF16), (MLA_KV_RANK, F32),
                  (MLA_ROPE, F32), (LANES, F32), (LANES, F32), (FOX_HEADS, F32), (GLA_PACK, F32)]
    return pl.pallas_call(
        functools.partial(_premix_kernel, tiles_per_seq=tiles_per_seq, per_token=per_token),
        grid=(nt,),
        in_specs=[tok(D_MODEL), mod_spec, mod_spec, pos_spec, pos_spec] + [full(c) for c in consts],
        out_specs=[tok(w) for w, _ in out_widths],
        out_shape=[jax.ShapeDtypeStruct((t, w), dt) for w, dt in out_widths],
        scratch_shapes=[pltpu.VMEM((8, LANES), F32)],
        compiler_params=_cparams(1), name="premix",
    )(x2d, shift, scale, cos_t, sin_t, *consts)


def _flash_kernel(q_ref, k_ref, v_ref, o_ref, *, tq):
    qi = pl.program_id(2)
    row = lax.broadcasted_iota(I32, (tq, tq), 0)
    col = lax.broadcasted_iota(I32, (tq, tq), 1)
    for hh in range(2):
        q = q_ref[0, :, hh * LANES:(hh + 1) * LANES]

        def step(kb, carry, diagonal):
            m, acc = carry
            start = pl.multiple_of(kb * tq, tq)
            k = k_ref[0, pl.ds(start, tq), hh * LANES:(hh + 1) * LANES]
            v = v_ref[0, pl.ds(start, tq), hh * LANES:(hh + 1) * LANES]
            s = _dot_nt(q, k)
            if diagonal:
                s = jnp.where(row >= col, s, -jnp.inf)
            m_new = jnp.maximum(m, jnp.max(s, axis=-1, keepdims=True))
            p = jnp.exp2(s - m_new)
            acc = jnp.exp2(m - m_new) * acc + _dot(p.astype(BF16), v)
            return m_new, acc

        init = (jnp.full((tq, 1), -jnp.inf, F32), jnp.zeros((tq, LANES), F32))
        carry = lax.fori_loop(0, qi, lambda kb, c: step(kb, c, False), init)
        _, acc = step(qi, carry, True)
        o_ref[0, :, hh * 64:(hh + 1) * 64] = (acc[:, 0:MLA_V] / acc[:, MLA_V:MLA_V + 1]).astype(BF16)


def _flash_call(q_all, k_all, v_all, *, tq):
    b, s, _ = q_all.shape
    return pl.pallas_call(
        functools.partial(_flash_kernel, tq=tq),
        grid=(b, ATT_HEADS // 2, s // tq),
        in_specs=[pl.BlockSpec((1, tq, 2 * LANES), lambda bi, p, qi: (bi, qi, p)),
                  pl.BlockSpec((1, s, 2 * LANES), lambda bi, p, qi: (bi, 0, p)),
                  pl.BlockSpec((1, s, 2 * LANES), lambda bi, p, qi: (bi, 0, p))],
        out_specs=pl.BlockSpec((1, tq, LANES), lambda bi, p, qi: (bi, qi, p)),
        out_shape=jax.ShapeDtypeStruct((b, s, ATT_HEADS * 64), BF16),
        compiler_params=_cparams(3), name="flash_attn",
    )(q_all, k_all, v_all)


def _gla_masks():
    r128 = lax.broadcasted_iota(I32, (GLA_HEADS * GLA_DK, GLA_HEADS * GLA_DV), 0)
    c256 = lax.broadcasted_iota(I32, (GLA_HEADS * GLA_DK, GLA_HEADS * GLA_DV), 1)
    return (r128 // GLA_DK) == (c256 // GLA_DV)


def _gla_out_norm(o, r, onorm):
    rr = lax.broadcasted_iota(I32, (256, 256), 0)
    cc = lax.broadcasted_iota(I32, (256, 256), 1)
    same = jnp.where((rr // GLA_DV) == (cc // GLA_DV), 1.0, 0.0).astype(BF16)
    ms = _dot((o * o).astype(BF16), same) * (1.0 / GLA_DV)
    return o * lax.rsqrt(ms + EPS) * onorm * _silu(r)


def _gla_kernel(g_ref, onorm_ref, o_ref, st_ref, sbd_ref, *, n_chunks):
    si = pl.program_id(1)

    @pl.when(si == 0)
    def _():
        sbd_ref[...] = jnp.zeros_like(sbd_ref)

    c = GLA_CHUNK
    lane128 = lax.broadcasted_iota(I32, (1, 128), 1)
    lane256 = lax.broadcasted_iota(I32, (1, 256), 1)
    bd = _gla_masks()
    rr = lax.broadcasted_iota(I32, (c, c), 0)
    cc = lax.broadcasted_iota(I32, (c, c), 1)
    tril = jnp.where(rr >= cc, 1.0, 0.0).astype(BF16)
    r4 = lax.broadcasted_iota(I32, (GLA_HEADS * c, c), 0)
    c4 = lax.broadcasted_iota(I32, (GLA_HEADS * c, c), 1)
    causal4 = (r4 % c) >= c4
    e_r = lax.broadcasted_iota(I32, (128, 128), 0)
    e_c = lax.broadcasted_iota(I32, (128, 128), 1)
    eye = e_r == e_c

    for ci in range(n_chunks):
        blk = g_ref[0, ci * c:(ci + 1) * c, :]
        q, k, la = blk[:, 0:128], blk[:, 128:256], blk[:, 256:384]
        v, r = blk[:, 384:640], blk[:, 640:896]
        b = _dot_split(tril, la, 2)
        b_end = b[c - 1:c, :]
        q_dec = q * jnp.exp(b)
        k_inv = (k * jnp.exp(-b)).astype(BF16)
        k_dec = (k * jnp.exp(b_end - b)).astype(BF16)
        vb = v.astype(BF16)
        s_bd = sbd_ref[...]
        o = _dot(q_dec.astype(BF16), s_bd.astype(BF16))
        q4 = jnp.concatenate([jnp.where((lane128 // GLA_DK) == hd, q_dec, 0.0) for hd in range(GLA_HEADS)], axis=0)
        att = jnp.where(causal4, _dot_nt(q4.astype(BF16), k_inv), 0.0)
        r_all = _dot(att.astype(BF16), vb)
        for hd in range(GLA_HEADS):
            o = o + jnp.where((lane256 // GLA_DV) == hd, r_all[hd * c:(hd + 1) * c, :], 0.0)
        o_ref[0, ci * c:(ci + 1) * c, :] = _gla_out_norm(o, r, onorm_ref[...]).astype(BF16)
        dcol = jnp.sum(jnp.where(eye, jnp.broadcast_to(jnp.exp(b_end), (128, 128)), 0.0), axis=1, keepdims=True)
        sbd_ref[...] = s_bd * dcol + jnp.where(bd, _dot_tn(k_dec, vb), 0.0)

    @pl.when(si == pl.num_programs(1) - 1)
    def _():
        s_bd = sbd_ref[...]
        for hd in range(GLA_HEADS):
            st_ref[0, hd] = s_bd[hd * GLA_DK:(hd + 1) * GLA_DK, hd * GLA_DV:(hd + 1) * GLA_DV]


def _gla_call(gla_in, onorm, *, sup):
    b, s, _ = gla_in.shape
    return pl.pallas_call(
        functools.partial(_gla_kernel, n_chunks=sup // GLA_CHUNK),
        grid=(b, s // sup),
        in_specs=[pl.BlockSpec((1, sup, GLA_PACK), lambda bi, si: (bi, si, 0)),
                  pl.BlockSpec((1, 256), lambda bi, si: (0, 0))],
        out_specs=[pl.BlockSpec((1, sup, 256), lambda bi, si: (bi, si, 0)),
                   pl.BlockSpec((1, GLA_HEADS, GLA_DK, GLA_DV), lambda bi, si: (bi, 0, 0, 0))],
        out_shape=[jax.ShapeDtypeStruct((b, s, 256), BF16),
                   jax.ShapeDtypeStruct((b, GLA_HEADS, GLA_DK, GLA_DV), F32)],
        scratch_shapes=[pltpu.VMEM((GLA_HEADS * GLA_DK, GLA_HEADS * GLA_DV), F32)],
        compiler_params=_cparams(2), name="gla_chunked",
    )(gla_in, onorm)


def _gla_step_kernel(g_ref, s0_ref, onorm_ref, o_ref, st_ref, *, nb):
    lane128 = lax.broadcasted_iota(I32, (1, 128), 1)
    lane256 = lax.broadcasted_iota(I32, (1, 256), 1)
    bd = _gla_masks()
    e_r = lax.broadcasted_iota(I32, (128, 128), 0)
    e_c = lax.broadcasted_iota(I32, (128, 128), 1)
    eye = e_r == e_c
    rows = []
    for j in range(nb):
        blk = g_ref[j:j + 1, :]
        q, k, la = blk[:, 0:128], blk[:, 128:256], blk[:, 256:384]
        v = blk[:, 384:640]
        s0 = s0_ref[j]
        s_st = s0.reshape(GLA_HEADS * GLA_DK, GLA_DV)
        s_bd = jnp.where(bd, jnp.concatenate([s_st] * GLA_HEADS, axis=1), 0.0)
        ea = jnp.exp(la)
        o = _dot((q * ea).astype(BF16), s_bd.astype(BF16))
        qk = q * k
        for hd in range(GLA_HEADS):
            dot_h = jnp.sum(jnp.where((lane128 // GLA_DK) == hd, qk, 0.0), axis=-1, keepdims=True)
            o = o + jnp.where((lane256 // GLA_DV) == hd, dot_h * v, 0.0)
        rows.append(o)
        ecol = jnp.sum(jnp.where(eye, jnp.broadcast_to(ea, (128, 128)), 0.0), axis=1, keepdims=True)
        kcol = jnp.sum(jnp.where(eye, jnp.broadcast_to(k, (128, 128)), 0.0), axis=1, keepdims=True)
        s_new = s_bd * ecol + jnp.where(bd, kcol * v, 0.0)
        for hd in range(GLA_HEADS):
            st_ref[j, hd] = s_new[hd * GLA_DK:(hd + 1) * GLA_DK, hd * GLA_DV:(hd + 1) * GLA_DV]
    o_all = jnp.concatenate(rows, axis=0)
    o_ref[...] = _gla_out_norm(o_all, g_ref[:, 640:896], onorm_ref[...]).astype(BF16)


def _gla_step_call(gla_in, state, onorm):
    t = gla_in.shape[0]
    nb = 8
    return pl.pallas_call(
        functools.partial(_gla_step_kernel, nb=nb),
        grid=(t // nb,),
        in_specs=[pl.BlockSpec((nb, GLA_PACK), lambda i: (i, 0)),
                  pl.BlockSpec((nb, GLA_HEADS, GLA_DK, GLA_DV), lambda i: (i, 0, 0, 0)),
                  pl.BlockSpec((1, 256), lambda i: (0, 0))],
        out_specs=[pl.BlockSpec((nb, 256), lambda i: (i, 0)),
                   pl.BlockSpec((nb, GLA_HEADS, GLA_DK, GLA_DV), lambda i: (i, 0, 0, 0))],
        out_shape=[jax.ShapeDtypeStruct((t, 256), BF16),
                   jax.ShapeDtypeStruct((t, GLA_HEADS, GLA_DK, GLA_DV), F32)],
        compiler_params=_cparams(1), name="gla_step",
    )(gla_in, state, onorm)


def _post_kernel(og_ref, oa_ref, wout_ref, x_ref, gm_ref, gffn_ref, sh_ref, sc_ref, rw_ref, rb_ref,
                 xmid_ref, t_ref, ew_ref, pos_ref, cnt_ref, *, per_token):
    tm = x_ref.shape[0]
    gm, sh, sc = (gm_ref[...], sh_ref[...], sc_ref[...]) if per_token else (gm_ref[0], sh_ref[0], sc_ref[0])
    mix = _dot(og_ref[...], wout_ref[0:256, :]) + _dot(oa_ref[...], wout_ref[256:1024, :])
    xm = x_ref[...] + gm * mix
    xmid_ref[...] = xm
    t = _rms(xm, gffn_ref[...]) * (1.0 + sc) + sh
    t_ref[...] = t.astype(BF16)

    t_hi, t_lo = _split(t, 2)
    logits = _dot(t_hi, rw_ref[0]) + _dot(t_hi, rw_ref[1]) + _dot(t_lo, rw_ref[0])
    lt = jnp.transpose(logits)[0:N_EXPERTS, :]
    scores = 1.0 / (1.0 + jnp.exp(-lt))
    biased = scores + rb_ref[...]
    sc_rows = [scores[e:e + 1, :] for e in range(N_EXPERTS)]
    b_rows = [biased[e:e + 1, :] for e in range(N_EXPERTS)]

    def first_max(vals):
        m = vals[0]
        for a in vals[1:]:
            m = jnp.maximum(m, a)
        idx = jnp.full(m.shape, len(vals) - 1, I32)
        for j in range(len(vals) - 2, -1, -1):
            idx = jnp.where(vals[j] == m, j, idx)
        return m, idx

    gsum, loc1, loc2 = [], [], []
    for g in range(N_GROUPS):
        a = b_rows[g * EXPERTS_PER_GROUP:(g + 1) * EXPERTS_PER_GROUP]
        m1, i1 = first_max(a)
        a2 = [jnp.where(i1 == j, -jnp.inf, a[j]) for j in range(EXPERTS_PER_GROUP)]
        m2, i2 = first_max(a2)
        gsum.append(m1 + m2)
        loc1.append(i1)
        loc2.append(i2)
    _, gsel = first_max(gsum)
    l1, l2 = loc1[N_GROUPS - 1], loc2[N_GROUPS - 1]
    for g in range(N_GROUPS - 2, -1, -1):
        l1 = jnp.where(gsel == g, loc1[g], l1)
        l2 = jnp.where(gsel == g, loc2[g], l2)
    e1 = gsel * EXPERTS_PER_GROUP + l1
    e2 = gsel * EXPERTS_PER_GROUP + l2
    w1 = jnp.zeros_like(sc_rows[0])
    w2 = jnp.zeros_like(sc_rows[0])
    for e in range(N_EXPERTS):
        w1 = jnp.where(e1 == e, sc_rows[e], w1)
        w2 = jnp.where(e2 == e, sc_rows[e], w2)
    den = w1 + w2
    ew_ref[0:1, :] = w1 / den
    ew_ref[1:2, :] = w2 / den

    sub = lax.broadcasted_iota(I32, (N_EXPERTS, tm), 0)
    rr = lax.broadcasted_iota(I32, (tm, tm), 0)
    cc = lax.broadcasted_iota(I32, (tm, tm), 1)
    triu = jnp.where(rr <= cc, 1.0, 0.0).astype(BF16)
    oh1 = jnp.where(sub == e1, 1.0, 0.0)
    oh2 = jnp.where(sub == e2, 1.0, 0.0)
    cum1 = _dot(oh1.astype(BF16), triu)
    cum2 = _dot(oh2.astype(BF16), triu)
    tot1 = cum1[:, tm - 1:tm]
    n8 = jnp.floor((tot1 + cum2[:, tm - 1:tm] + 7.0) * 0.125) * 8.0
    n8b = jnp.broadcast_to(n8, (N_EXPERTS, LANES))
    er = lax.broadcasted_iota(I32, (N_EXPERTS, N_EXPERTS), 0)
    ec = lax.broadcasted_iota(I32, (N_EXPERTS, N_EXPERTS), 1)
    off = _dot_split(jnp.where(ec < er, 1.0, 0.0).astype(BF16), n8b, 2)[:, 0:1]
    pos1 = jnp.sum(oh1 * (off + cum1 - oh1), axis=0, keepdims=True)
    pos2 = jnp.sum(oh2 * (off + tot1 + cum2 - oh2), axis=0, keepdims=True)
    pos_ref[0:1, :] = pos1.astype(I32)
    pos_ref[1:2, :] = pos2.astype(I32)
    cnt_ref[0] = n8b


def _post_call(o_gla, o_attn, w_out, x2d, gate, gffn, shift, scale, rw_pieces, rb_col, *, tm, tiles_per_seq, per_token):
    t = x2d.shape[0]
    nt = t // tm
    full = lambda a: pl.BlockSpec(a.shape, lambda i: (0,) * a.ndim)
    tok = lambda w: pl.BlockSpec((tm, w), lambda i: (i, 0))
    mod_spec = tok(D_MODEL) if per_token else pl.BlockSpec((1, 1, D_MODEL), lambda i: (i // tiles_per_seq, 0, 0))
    rowblk = pl.BlockSpec((2, tm), lambda i: (0, i))
    return pl.pallas_call(
        functools.partial(_post_kernel, per_token=per_token),
        grid=(nt,),
        in_specs=[tok(256), tok(768), full(w_out), tok(D_MODEL), mod_spec, full(gffn), mod_spec, mod_spec,
                  full(rw_pieces), full(rb_col)],
        out_specs=[tok(D_MODEL), tok(D_MODEL), rowblk, rowblk,
                   pl.BlockSpec((1, N_EXPERTS, LANES), lambda i: (i, 0, 0))],
        out_shape=[jax.ShapeDtypeStruct((t, D_MODEL), F32), jax.ShapeDtypeStruct((t, D_MODEL), BF16),
                   jax.ShapeDtypeStruct((2, t), F32), jax.ShapeDtypeStruct((2, t), I32),
                   jax.ShapeDtypeStruct((nt, N_EXPERTS, LANES), F32)],
        compiler_params=_cparams(1), name="post_mix_route",
    )(o_gla, o_attn, w_out, x2d, gate, gffn, shift, scale, rw_pieces, rb_col)


SEG_ALIGN = 8


def _local_rows(tm):
    return 2 * tm + N_EXPERTS * SEG_ALIGN


def _segment_copies(segn_ref, sego_ref, tile, local, sem, *, remote, tm, to_remote, wait):
    nbits = (2 * tm // SEG_ALIGN).bit_length()
    lo = 0
    for e in range(N_EXPERTS):
        n = segn_ref[tile * N_EXPERTS + e]
        g = sego_ref[tile * N_EXPERTS + e]
        for b in reversed(range(nbits)):
            size = SEG_ALIGN << b
            above = (n >> (b + 4)) << (b + 4)

            @pl.when((n & size) != 0)
            def _(lo=lo, g=g, above=above, size=size):
                lrows = local.at[pl.ds(pl.multiple_of(lo + above, SEG_ALIGN), size)]
                rrows = remote.at[pl.ds(pl.multiple_of(g + above, SEG_ALIGN), size)]
                cp = pltpu.make_async_copy(lrows, rrows, sem) if to_remote else pltpu.make_async_copy(rrows, lrows, sem)
                if wait:
                    cp.wait()
                else:
                    cp.start()
        lo = lo + n


def _dispatch_kernel(segn_ref, sego_ref, pos_ref, t_ref, xs_in_ref, xs_ref, stage, sem, *, tm):
    del xs_in_ref
    i = pl.program_id(0)
    n = pl.num_programs(0)
    slot = i % 2
    seg = functools.partial(_segment_copies, segn_ref, sego_ref, remote=xs_ref, tm=tm, to_remote=True)

    @pl.when(i > 0)
    def _():
        seg(i - 1, stage.at[1 - slot], sem.at[1 - slot], wait=True)

    nsl = _local_rows(tm)
    row = lax.broadcasted_iota(I32, (nsl, tm), 0)
    perm = jnp.where(row == pos_ref[0:1, :], 1.0, jnp.where(row == pos_ref[1:2, :], 1.0, 0.0)).astype(BF16)
    stage[slot] = _pack_rows(_dot(perm, t_ref[...]))
    seg(i, stage.at[slot], sem.at[slot], wait=False)

    @pl.when(i == n - 1)
    def _():
        seg(i, stage.at[slot], sem.at[slot], wait=True)


def _dispatch_call(seg_n, seg_off, pos, tb, n_slots, *, tm):
    t = tb.shape[0]
    zeros = jnp.zeros((n_slots, 512), U32)
    grid_spec = pltpu.PrefetchScalarGridSpec(
        num_scalar_prefetch=2, grid=(t // tm,),
        in_specs=[pl.BlockSpec((2, tm), lambda i, sn, so: (0, i)),
                  pl.BlockSpec((tm, D_MODEL), lambda i, sn, so: (i, 0)),
                  pl.BlockSpec(memory_space=pl.ANY)],
        out_specs=pl.BlockSpec(memory_space=pl.ANY),
        scratch_shapes=[pltpu.VMEM((2, _local_rows(tm), 512), U32), pltpu.SemaphoreType.DMA((2,))])
    return pl.pallas_call(
        functools.partial(_dispatch_kernel, tm=tm), grid_spec=grid_spec,
        out_shape=jax.ShapeDtypeStruct((n_slots, 512), U32),
        input_output_aliases={4: 0},
        compiler_params=_cparams(1), name="moe_dispatch",
    )(seg_n, seg_off, pos, tb, zeros)


def _expert_kernel(be_ref, nu_ref, xs_ref, w1_ref, w3_ref, w2_ref, ys_ref):
    j = pl.program_id(0)

    @pl.when(j < nu_ref[0])
    def _():
        xa, xb = _unpack_rows(xs_ref[...])
        xa, xb = xa.astype(BF16), xb.astype(BF16)
        h1 = _dot(xa, w1_ref[0, 0:512, :]) + _dot(xb, w1_ref[0, 512:1024, :])
        h3 = _dot(xa, w3_ref[0, 0:512, :]) + _dot(xb, w3_ref[0, 512:1024, :])
        y = _dot((_silu(h1) * h3).astype(BF16), w2_ref[0])
        ys_ref[...] = _pack_rows(y)

    @pl.when(j >= nu_ref[0])
    def _():
        ys_ref[...] = jnp.zeros_like(ys_ref)


def _expert_call(block_expert, n_used, xs, w1, w3, w2):
    n_slots = xs.shape[0]
    nb = n_slots // MOE_BLK
    grid_spec = pltpu.PrefetchScalarGridSpec(
        num_scalar_prefetch=2, grid=(nb,),
        in_specs=[pl.BlockSpec((MOE_BLK, 512), lambda j, be, nu: (j, 0)),
                  pl.BlockSpec((1, D_MODEL, D_EXPERT), lambda j, be, nu: (be[j], 0, 0)),
                  pl.BlockSpec((1, D_MODEL, D_EXPERT), lambda j, be, nu: (be[j], 0, 0)),
                  pl.BlockSpec((1, D_EXPERT, D_MODEL), lambda j, be, nu: (be[j], 0, 0))],
        out_specs=pl.BlockSpec((MOE_BLK, 512), lambda j, be, nu: (j, 0)))
    return pl.pallas_call(
        _expert_kernel, grid_spec=grid_spec,
        out_shape=jax.ShapeDtypeStruct((n_slots, 512), U32),
        compiler_params=_cparams(1), name="moe_experts",
    )(block_expert, n_used, xs, w1, w3, w2)


def _combine_kernel(segn_ref, sego_ref, ys_ref, xmid_ref, gc_ref, wcol_ref, pcol_ref, o_ref, ybuf, sem, *, tm, per_token):
    i = pl.program_id(0)
    n = pl.num_programs(0)
    slot = i % 2
    seg = functools.partial(_segment_copies, segn_ref, sego_ref, remote=ys_ref, tm=tm, to_remote=False)

    @pl.when(i == 0)
    def _():
        ybuf[...] = jnp.zeros_like(ybuf)
        seg(0, ybuf.at[0], sem.at[0], wait=False)

    @pl.when(i + 1 < n)
    def _():
        seg(i + 1, ybuf.at[1 - slot], sem.at[1 - slot], wait=False)

    seg(i, ybuf.at[slot], sem.at[slot], wait=True)

    nsl = _local_rows(tm)
    lane = lax.broadcasted_iota(I32, (tm, nsl), 1)
    w, p = wcol_ref[...], pcol_ref[...]
    wm = jnp.where(lane == p[:, 0:1], w[:, 0:1], 0.0) + jnp.where(lane == p[:, 1:2], w[:, 1:2], 0.0)
    w_hi, w_lo = _split(wm, 2)
    ya, yb = _unpack_rows(ybuf[slot])
    ya, yb = ya.astype(BF16), yb.astype(BF16)
    gc = gc_ref[...] if per_token else gc_ref[0]
    o_ref[:, 0:512] = xmid_ref[:, 0:512] + gc[:, 0:512] * (_dot(w_hi, ya) + _dot(w_lo, ya))
    o_ref[:, 512:1024] = xmid_ref[:, 512:1024] + gc[:, 512:1024] * (_dot(w_hi, yb) + _dot(w_lo, yb))


def _combine_call(seg_n, seg_off, ys, xmid, gate, wcol, pcol, *, tm, tiles_per_seq, per_token):
    t = xmid.shape[0]
    mod_spec = (pl.BlockSpec((tm, D_MODEL), lambda i, sn, so: (i, 0)) if per_token
                else pl.BlockSpec((1, 1, D_MODEL), lambda i, sn, so: (i // tiles_per_seq, 0, 0)))
    grid_spec = pltpu.PrefetchScalarGridSpec(
        num_scalar_prefetch=2, grid=(t // tm,),
        in_specs=[pl.BlockSpec(memory_space=pl.ANY),
                  pl.BlockSpec((tm, D_MODEL), lambda i, sn, so: (i, 0)), mod_spec,
                  pl.BlockSpec((tm, 2), lambda i, sn, so: (i, 0)),
                  pl.BlockSpec((tm, 2), lambda i, sn, so: (i, 0))],
        out_specs=pl.BlockSpec((tm, D_MODEL), lambda i, sn, so: (i, 0)),
        scratch_shapes=[pltpu.VMEM((2, _local_rows(tm), 512), U32), pltpu.SemaphoreType.DMA((2,))])
    return pl.pallas_call(
        functools.partial(_combine_kernel, tm=tm, per_token=per_token), grid_spec=grid_spec,
        out_shape=jax.ShapeDtypeStruct((t, D_MODEL), F32),
        compiler_params=_cparams(1), name="moe_combine",
    )(seg_n, seg_off, ys, xmid, gate, wcol, pcol)


def _moe(tb, ew, pos, cnt, xmid, gate, w1, w3, w2, *, tm, tiles_per_seq, per_token):
    t = xmid.shape[0]
    nt = t // tm
    n8 = cnt[:, :, 0].astype(I32)
    total = jnp.sum(n8, axis=0)
    padded = (total + MOE_BLK - 1) // MOE_BLK * MOE_BLK
    pend = jnp.cumsum(padded)
    seg_off = ((pend - padded)[None, :] + jnp.cumsum(n8, axis=0) - n8).astype(I32).reshape(-1)
    seg_n = n8.reshape(-1)
    nb = -(-(2 * t + nt * N_EXPERTS * (SEG_ALIGN - 1)) // MOE_BLK) + N_EXPERTS
    block_expert = jnp.clip(jnp.searchsorted(pend, jnp.arange(nb, dtype=I32) * MOE_BLK, side="right"),
                            0, N_EXPERTS - 1).astype(I32)
    n_used = (pend[-1:] // MOE_BLK).astype(I32)
    xs = _dispatch_call(seg_n, seg_off, pos, tb, nb * MOE_BLK, tm=tm)
    ys = _expert_call(block_expert, n_used, xs, w1, w3, w2)
    return _combine_call(seg_n, seg_off, ys, xmid, gate, jnp.transpose(ew), jnp.transpose(pos), tm=tm,
                         tiles_per_seq=tiles_per_seq, per_token=per_token)


def _qprep_kernel(qall_ref, wukt_ref, ql_ref, qr_ref, qf_ref):
    for hd in range(MLA_HEADS):
        blk = qall_ref[:, hd * LANES:(hd + 1) * LANES]
        ql_ref[:, hd, :] = _dot(blk, wukt_ref[hd])
        qr_ref[:, hd, :] = blk[:, MLA_NOPE:MLA_QK].astype(F32)
    for hd in range(FOX_HEADS):
        qf_ref[:, hd, :] = qall_ref[:, (MLA_HEADS + hd) * LANES:(MLA_HEADS + hd) * LANES + FOX_DH].astype(F32)


def _qprep_call(q_all, wukt_p):
    t = q_all.shape[0]
    return pl.pallas_call(
        _qprep_kernel,
        out_shape=[jax.ShapeDtypeStruct((t, MLA_HEADS, MLA_KV_RANK), F32),
                   jax.ShapeDtypeStruct((t, MLA_HEADS, MLA_ROPE), F32),
                   jax.ShapeDtypeStruct((t, FOX_HEADS, FOX_DH), F32)],
        compiler_params=pltpu.CompilerParams(vmem_limit_bytes=VMEM_LIMIT), name="sample_qprep",
    )(q_all, wukt_p)


def _foxbias_kernel(pt_ref, lf_hbm, o_ref, buf, sem, *, layer, n_pages):
    b = pl.program_id(0)
    nb = pl.num_programs(0)

    def copies(bb, slot):
        return [pltpu.make_async_copy(lf_hbm.at[layer, pt_ref[bb, p]], buf.at[slot, p], sem.at[slot])
                for p in range(n_pages)]

    @pl.when(b == 0)
    def _():
        for cp in copies(0, 0):
            cp.start()

    slot = b % 2

    @pl.when(b + 1 < nb)
    def _():
        for cp in copies(b + 1, 1 - slot):
            cp.start()

    for cp in copies(b, slot):
        cp.wait()

    rr = lax.broadcasted_iota(I32, (PAGE_SIZE, PAGE_SIZE), 0)
    cc = lax.broadcasted_iota(I32, (PAGE_SIZE, PAGE_SIZE), 1)
    m_suf = jnp.where(rr > cc, 1.0, 0.0).astype(BF16)
    m_tot = jnp.ones((PAGE_SIZE, PAGE_SIZE), BF16)
    pr = lax.broadcasted_iota(I32, (n_pages, n_pages), 0)
    pc = lax.broadcasted_iota(I32, (n_pages, n_pages), 1)
    later = jnp.where(pc > pr, 1.0, 0.0).astype(BF16)
    for hd in range(FOX_HEADS):
        pieces = _split(buf[slot, :, hd, :], 3)
        within = sum(_dot(p, m_suf) for p in pieces)
        total = sum(_dot(p, m_tot) for p in pieces)
        o_ref[0, hd] = within + _dot_split(later, total, 3)


def _foxbias_call(page_table, lf_cache, layer):
    nb, n_pages = page_table.shape
    grid_spec = pltpu.PrefetchScalarGridSpec(
        num_scalar_prefetch=1, grid=(nb,),
        in_specs=[pl.BlockSpec(memory_space=pl.ANY)],
        out_specs=pl.BlockSpec((1, FOX_HEADS, n_pages, PAGE_SIZE), lambda b, pt: (b, 0, 0, 0)),
        scratch_shapes=[pltpu.VMEM((2, n_pages, FOX_HEADS, PAGE_SIZE), F32), pltpu.SemaphoreType.DMA((2,))])
    return pl.pallas_call(
        functools.partial(_foxbias_kernel, layer=layer, n_pages=n_pages), grid_spec=grid_spec,
        out_shape=jax.ShapeDtypeStruct((nb, FOX_HEADS, n_pages, PAGE_SIZE), F32),
        compiler_params=_cparams(1), name="fox_past_bias",
    )(page_table, lf_cache)


def _paged_kernel(pt_ref, ql_ref, qr_ref, qf_ref, bias_ref, gnew_ref, ckvn_ref, krn_ref, fkn_ref, fvn_ref,
                  ckv_hbm, kr_hbm, fk_hbm, fv_hbm, ol_ref, of_ref,
                  ckv_buf, kr_buf, fk_buf, fv_buf, sem, m_ref, l_ref, acc_ref, mf_ref, lf_ref, accf_ref,
                  *, layer, n_chunks, pg):
    s = pl.program_id(0)
    ns = pl.num_programs(0)
    chunk = s % n_chunks

    def copies(step, slot):
        bb, ch = step // n_chunks, step % n_chunks
        out = []
        for p in range(pg):
            phys = pt_ref[bb, ch * pg + p]
            keys = pl.ds(p * PAGE_SIZE, PAGE_SIZE)
            out.append(pltpu.make_async_copy(ckv_hbm.at[layer, phys], ckv_buf.at[slot, keys], sem.at[slot, 0]))
            out.append(pltpu.make_async_copy(kr_hbm.at[layer, phys], kr_buf.at[slot, :, keys], sem.at[slot, 1]))
            out.append(pltpu.make_async_copy(fk_hbm.at[layer, phys], fk_buf.at[slot, :, :, keys], sem.at[slot, 2]))
            out.append(pltpu.make_async_copy(fv_hbm.at[layer, phys], fv_buf.at[slot, :, :, keys], sem.at[slot, 3]))
        return out

    @pl.when(s == 0)
    def _():
        for cp in copies(0, 0):
            cp.start()

    slot = s % 2

    @pl.when(s + 1 < ns)
    def _():
        for cp in copies(s + 1, 1 - slot):
            cp.start()

    @pl.when(chunk == 0)
    def _():
        m_ref[...] = jnp.full_like(m_ref, -1e30)
        l_ref[...] = jnp.zeros_like(l_ref)
        acc_ref[...] = jnp.zeros_like(acc_ref)
        mf_ref[...] = jnp.full_like(mf_ref, -1e30)
        lf_ref[...] = jnp.zeros_like(lf_ref)
        accf_ref[...] = jnp.zeros_like(accf_ref)

    for cp in copies(s, slot):
        cp.wait()

    def merge(m_r, l_r, acc_r, sc, pv):
        m_old = m_r[:, 0:1]
        m_new = jnp.maximum(m_old, jnp.max(sc, axis=-1, keepdims=True))
        alpha = jnp.exp2(m_old - m_new)
        p = jnp.exp2(sc - m_new)
        l_r[...] = jnp.broadcast_to(alpha * l_r[:, 0:1] + jnp.sum(p, axis=-1, keepdims=True), l_r.shape)
        acc_r[...] = alpha * acc_r[...] + pv(p)
        m_r[...] = jnp.broadcast_to(m_new, m_r.shape)

    ql, qr, qf = ql_ref[0], qr_ref[0], qf_ref[0]
    ckv = ckv_buf[slot]
    merge(m_ref, l_ref, acc_ref, _dot_nt(ql, ckv) + _dot(qr, kr_buf[slot]), lambda p: _dot(p, ckv))
    sf = jnp.concatenate([_dot(qf[n * FOX_GROUP:(n + 1) * FOX_GROUP], fk_buf[slot, n])
                          for n in range(FOX_KV_HEADS)], axis=0)
    merge(mf_ref, lf_ref, accf_ref, sf + (bias_ref[0] + gnew_ref[0]) * LOG2E,
          lambda p: jnp.concatenate([_dot_nt(p[n * FOX_GROUP:(n + 1) * FOX_GROUP], fv_buf[slot, n])
                                     for n in range(FOX_KV_HEADS)], axis=0))

    @pl.when(chunk == n_chunks - 1)
    def _():
        ckvn = ckvn_ref[0]
        s_new = jnp.sum(ql * ckvn, axis=-1, keepdims=True) + jnp.sum(qr * krn_ref[0], axis=-1, keepdims=True)
        merge(m_ref, l_ref, acc_ref, s_new, lambda p: p * ckvn)
        ol_ref[0] = acc_ref[...] / l_ref[:, 0:1]
        rep = lambda a: jnp.concatenate([a[:, n * FOX_DH:(n + 1) * FOX_DH] for n in range(FOX_KV_HEADS)
                                         for _ in range(FOX_GROUP)], axis=0)
        fkn, fvn = rep(fkn_ref[0]), rep(fvn_ref[0])
        sf_new = jnp.sum(qf * fkn, axis=-1, keepdims=True)
        merge(mf_ref, lf_ref, accf_ref, sf_new, lambda p: p * fvn)
        of_ref[0] = accf_ref[...] / lf_ref[:, 0:1]


def _paged_call(page_table, ql, qr, qf, bias, gnew, ckvn, krn, fkn, fvn, c_ckv, c_kr, c_fk, c_fv, layer, *, pg):
    nb, n_pages = page_table.shape
    n_chunks = n_pages // pg
    kc = pg * PAGE_SIZE
    per_b = lambda shp: pl.BlockSpec((1,) + shp, lambda s, pt: (s // n_chunks,) + (0,) * len(shp))
    grid_spec = pltpu.PrefetchScalarGridSpec(
        num_scalar_prefetch=1, grid=(nb * n_chunks,),
        in_specs=[per_b((MLA_HEADS, MLA_KV_RANK)), per_b((MLA_HEADS, MLA_ROPE)), per_b((FOX_HEADS, FOX_DH)),
                  pl.BlockSpec((1, FOX_HEADS, kc), lambda s, pt: (s // n_chunks, 0, s % n_chunks)),
                  per_b((FOX_HEADS, 1)), per_b((1, MLA_KV_RANK)), per_b((1, MLA_ROPE)), per_b((1, LANES)),
                  per_b((1, LANES))] + [pl.BlockSpec(memory_space=pl.ANY)] * 4,
        out_specs=[per_b((MLA_HEADS, MLA_KV_RANK)), per_b((FOX_HEADS, FOX_DH))],
        scratch_shapes=[pltpu.VMEM((2, kc, MLA_KV_RANK), F32), pltpu.VMEM((2, MLA_ROPE, kc), F32),
                        pltpu.VMEM((2, FOX_KV_HEADS, FOX_DH, kc), F32), pltpu.VMEM((2, FOX_KV_HEADS, FOX_DH, kc), F32),
                        pltpu.SemaphoreType.DMA((2, 4)),
                        pltpu.VMEM((MLA_HEADS, LANES), F32), pltpu.VMEM((MLA_HEADS, LANES), F32),
                        pltpu.VMEM((MLA_HEADS, MLA_KV_RANK), F32),
                        pltpu.VMEM((FOX_HEADS, LANES), F32), pltpu.VMEM((FOX_HEADS, LANES), F32),
                        pltpu.VMEM((FOX_HEADS, FOX_DH), F32)])
    return pl.pallas_call(
        functools.partial(_paged_kernel, layer=layer, n_chunks=n_chunks, pg=pg), grid_spec=grid_spec,
        out_shape=[jax.ShapeDtypeStruct((nb, MLA_HEADS, MLA_KV_RANK), F32),
                   jax.ShapeDtypeStruct((nb, FOX_HEADS, FOX_DH), F32)],
        compiler_params=_cparams(1), name="paged_attn",
    )(page_table, ql, qr, qf, bias, gnew, ckvn, krn, fkn, fvn, c_ckv, c_kr, c_fk, c_fv)


def _oproj_kernel(ol_ref, of_ref, wuv_ref, o_ref):
    for hd in range(MLA_HEADS):
        o_ref[:, hd * MLA_V:(hd + 1) * MLA_V] = _dot(ol_ref[:, hd, :].astype(BF16), wuv_ref[hd]).astype(BF16)
    for hd in range(FOX_HEADS):
        o_ref[:, 512 + hd * FOX_DH:512 + (hd + 1) * FOX_DH] = of_ref[:, hd, :].astype(BF16)


def _oproj_call(o_lat, o_fox, wuv_h):
    t = o_lat.shape[0]
    return pl.pallas_call(
        _oproj_kernel, out_shape=jax.ShapeDtypeStruct((t, ATT_HEADS * 64), BF16),
        compiler_params=pltpu.CompilerParams(vmem_limit_bytes=VMEM_LIMIT), name="sample_oproj",
    )(o_lat, o_fox, wuv_h)


def kernel(x_prompt, x_sample, cache_mla_ckv, cache_mla_krope, cache_fox_k, cache_fox_v, cache_fox_logf, state_gla, page_table, c_prompt, c_sample, router_w, router_b, attn_norm, ffn_norm, ada_w, ada_b, w_in, gla_wa2, gla_ba, gla_onorm, mla_qlat_norm, mla_wuq, mla_q_norm, mla_kv_norm, mla_kr_norm, mla_wuk, mla_wuv, fox_q_norm, fox_k_norm, fox_fb, w_out, moe_w1, moe_w3, moe_w2):
    bp, seq, _ = x_prompt.shape
    bs = x_sample.shape[0]
    depth = w_in.shape[0]
    n_pages = page_table.shape[1]
    tp = bp * seq
    tm_p = min(TM_PROMPT, seq)
    tps = seq // tm_p
    tm_m = min(TM_MOE, seq)
    tps_m = seq // tm_m
    tq = min(TQ, seq)
    sup = min(GLA_SUPER, seq)
    pg = min(PAGES_PER_STEP, n_pages)

    mods = _ada_call(jnp.concatenate([c_prompt, c_sample], axis=0), ada_w, ada_b)
    cos_p, sin_p = _rope_tables(jnp.arange(seq))
    cos_s, sin_s = _rope_tables(jnp.full((1,), n_pages * PAGE_SIZE))
    aug = _aug_constants()
    rw_pad = jnp.zeros((D_MODEL, LANES), F32).at[:, :N_EXPERTS].set(router_w)
    rw_hi = rw_pad.astype(BF16)
    rw_pieces = jnp.stack([rw_hi, (rw_pad - rw_hi.astype(F32)).astype(BF16)])
    rb_col = router_b.astype(F32)[:, None]
    c_kr = jnp.transpose(cache_mla_krope, (0, 1, 3, 2))
    c_fk = jnp.transpose(cache_fox_k, (0, 1, 3, 4, 2))
    c_fv = jnp.transpose(cache_fox_v, (0, 1, 3, 4, 2))
    c_lf = jnp.transpose(cache_fox_logf, (0, 1, 3, 2))

    xp = x_prompt.reshape(tp, D_MODEL)
    xs = x_sample.reshape(bs, D_MODEL)
    p_ent, s_ent = [], []
    for layer in range(depth):
        lw = _layer_weights(layer, w_in, gla_wa2, gla_ba, gla_onorm, mla_qlat_norm, mla_wuq, mla_q_norm,
                            mla_kv_norm, mla_kr_norm, mla_wuk, mla_wuv, fox_q_norm, fox_k_norm, fox_fb, attn_norm)
        wout_b = w_out[layer].astype(BF16)
        w1b, w3b, w2b = moe_w1[layer].astype(BF16), moe_w3[layer].astype(BF16), moe_w2[layer].astype(BF16)
        gffn = ffn_norm[layer][None, :]
        mod = mods[layer]
        mp = [mod[:bp, j * D_MODEL:(j + 1) * D_MODEL].reshape(bp, 1, D_MODEL) for j in range(6)]
        ms = [mod[bp:, j * D_MODEL:(j + 1) * D_MODEL] for j in range(6)]

        (q_all, k_all, v_all, ckv, kr, fk, fv, lf, gla_in) = _premix_call(
            xp, mp[0], mp[1], cos_p, sin_p, lw, aug, tm=tm_p, tiles_per_seq=tps, per_token=False)
        o_gla, st = _gla_call(gla_in.reshape(bp, seq, GLA_PACK), lw["onorm"], sup=sup)
        o_att = _flash_call(q_all.reshape(bp, seq, -1), k_all.reshape(bp, seq, -1), v_all.reshape(bp, seq, -1), tq=tq)
        p_ent.append((ckv.reshape(bp, seq, MLA_KV_RANK), kr.reshape(bp, seq, MLA_ROPE),
                      fk.reshape(bp, seq, FOX_KV_HEADS, FOX_DH), fv.reshape(bp, seq, FOX_KV_HEADS, FOX_DH),
                      lf.reshape(bp, seq, FOX_HEADS), st))
        xmid, tb, ew, pos, cnt = _post_call(
            o_gla.reshape(tp, 256), o_att.reshape(tp, -1), wout_b, xp, mp[2], gffn, mp[3], mp[4], rw_pieces, rb_col,
            tm=tm_m, tiles_per_seq=tps_m, per_token=False)
        xp = _moe(tb, ew, pos, cnt, xmid, mp[5], w1b, w3b, w2b, tm=tm_m, tiles_per_seq=tps_m, per_token=False)

        (q_all, _, _, ckv, kr, fk, fv, lf, gla_in) = _premix_call(
            xs, ms[0], ms[1], cos_s, sin_s, lw, aug, tm=bs, tiles_per_seq=1, per_token=True)
        o_gla, st = _gla_step_call(gla_in, state_gla[layer], lw["onorm"])
        wukt_p = jnp.concatenate([jnp.transpose(lw["wuk_raw"], (1, 2, 0)),
                                  jnp.zeros((MLA_HEADS, LANES - MLA_NOPE, MLA_KV_RANK), F32)], axis=1).astype(BF16)
        ql, qr, qf = _qprep_call(q_all, wukt_p)
        bias = _foxbias_call(page_table, c_lf, layer).reshape(bs, FOX_HEADS, n_pages * PAGE_SIZE)
        o_lat, o_fox = _paged_call(page_table, ql, qr, qf, bias, lf.reshape(bs, FOX_HEADS, 1),
                                   ckv.reshape(bs, 1, -1), kr.reshape(bs, 1, -1), fk.reshape(bs, 1, -1),
                                   fv.reshape(bs, 1, -1), cache_mla_ckv, c_kr, c_fk, c_fv, layer, pg=pg)
        o_att = _oproj_call(o_lat, o_fox, jnp.transpose(lw["wuv_raw"], (1, 0, 2)).astype(BF16))
        s_ent.append((ckv.reshape(bs, 1, MLA_KV_RANK), kr.reshape(bs, 1, MLA_ROPE),
                      fk.reshape(bs, 1, FOX_KV_HEADS, FOX_DH), fv.reshape(bs, 1, FOX_KV_HEADS, FOX_DH),
                      lf.reshape(bs, 1, FOX_HEADS), st))
        xmid, tb, ew, pos, cnt = _post_call(
            o_gla, o_att, wout_b, xs, ms[2], gffn, ms[3], ms[4], rw_pieces, rb_col,
            tm=bs, tiles_per_seq=1, per_token=True)
        xs = _moe(tb, ew, pos, cnt, xmid, ms[5], w1b, w3b, w2b, tm=bs, tiles_per_seq=1, per_token=True)

    stack = lambda ents, j: jnp.stack([e[j] for e in ents], axis=0)
    return (xp.reshape(bp, seq, D_MODEL), xs.reshape(bs, 1, D_MODEL),
            stack(p_ent, 0), stack(p_ent, 1), stack(p_ent, 2), stack(p_ent, 3), stack(p_ent, 4), stack(p_ent, 5),
            stack(s_ent, 0), stack(s_ent, 1), stack(s_ent, 2), stack(s_ent, 3), stack(s_ent, 4), stack(s_ent, 5))
```

```python
import functools

import numpy as np
import jax
import jax.numpy as jnp
from jax import lax
from jax.experimental import pallas as pl
from jax.experimental.pallas import tpu as pltpu

F32, BF16, I32, U32 = jnp.float32, jnp.bfloat16, jnp.int32, jnp.uint32

D_MODEL = 1024
EPS = 1e-6
GLA_HEADS, GLA_DK, GLA_DV, GLA_RANK, GLA_TAU, GLA_CHUNK = 4, 32, 64, 16, 16.0, 64
MLA_HEADS, MLA_Q_RANK, MLA_KV_RANK, MLA_NOPE, MLA_ROPE, MLA_V = 8, 384, 256, 64, 32, 64
MLA_QK = MLA_NOPE + MLA_ROPE
MLA_SCALE = MLA_QK ** -0.5
ROPE_THETA = 10000.0
FOX_HEADS, FOX_KV_HEADS, FOX_GROUP, FOX_DH = 4, 2, 2, 64
FOX_SCALE = FOX_DH ** -0.5
N_EXPERTS, N_GROUPS, EXPERTS_PER_GROUP, D_EXPERT = 16, 4, 4, 512
PAGE_SIZE = 128
IN_SIZES = (128, 128, 256, 16, 256, 384, 256, 32, 256, 128, 128, 4)
ATT_HEADS = MLA_HEADS + FOX_HEADS
LOG2E = 1.4426950408889634

LANES = 128
VMEM_LIMIT = 56 * 1024 * 1024

O_GQ, O_GK, O_GV, O_GR, O_CQ, O_CKV, O_FQ, O_FK, O_FV, O_SM = 0, 128, 256, 512, 768, 1152, 1408, 1920, 2176, 2432
W_IN_PACKED = 2560
SM_GG, SM_FF, SM_KR = 0, 16, 64
GLA_PACK = 896

TM_PROMPT = 512
TM_MOE = 256
TQ = 512
GLA_SUPER = 512
MOE_BLK = 512
PAGES_PER_STEP = 32


def _cparams(n_axes):
    return pltpu.CompilerParams(dimension_semantics=("arbitrary",) * n_axes, vmem_limit_bytes=VMEM_LIMIT)


def _dot(a, b):
    return jnp.dot(a, b, preferred_element_type=F32)


def _dot_nt(a, b):
    return lax.dot_general(a, b, (((1,), (1,)), ((), ())), preferred_element_type=F32)


def _dot_tn(a, b):
    return lax.dot_general(a, b, (((0,), (0,)), ((), ())), preferred_element_type=F32)


def _split(x, n):
    out, r = [], x
    for _ in range(n):
        p = r.astype(BF16)
        out.append(p)
        r = r - p.astype(F32)
    return out


def _dot_split(a_exact, b, n):
    acc = None
    for p in _split(b, n):
        t = _dot(a_exact, p)
        acc = t if acc is None else acc + t
    return acc


def _logsig(x):
    return jnp.minimum(x, 0.0) - jnp.log1p(jnp.exp(-jnp.abs(x)))


def _silu(x):
    return x / (1.0 + jnp.exp(-x))


def _rms(x, gain):
    return x * lax.rsqrt(jnp.mean(x * x, axis=-1, keepdims=True) + EPS) * gain


def _pack_rows(x):
    hi = pltpu.bitcast(x[:, :512].astype(BF16).astype(F32), U32)
    lo = pltpu.bitcast(x[:, 512:].astype(BF16).astype(F32), U32) >> 16
    return hi | lo


def _unpack_rows(w):
    return pltpu.bitcast(w & jnp.uint32(0xFFFF0000), F32), pltpu.bitcast(w << 16, F32)


def _ada_kernel(c_ref, w_ref, b_ref, o_ref):
    s_hi, s_lo = _split(_silu(c_ref[...]), 2)
    w_hi, w_lo = _split(w_ref[0], 2)
    o_ref[0] = _dot(s_hi, w_hi) + _dot(s_hi, w_lo) + _dot(s_lo, w_hi) + b_ref[0]


def _ada_call(c_all, ada_w, ada_b):
    rows, depth, n = c_all.shape[0], ada_w.shape[0], ada_w.shape[2]
    tn = n // 4
    return pl.pallas_call(
        _ada_kernel, grid=(depth, n // tn),
        in_specs=[pl.BlockSpec((rows, D_MODEL), lambda l, j: (0, 0)),
                  pl.BlockSpec((1, D_MODEL, tn), lambda l, j: (l, 0, j)),
                  pl.BlockSpec((1, 1, tn), lambda l, j: (l, 0, j))],
        out_specs=pl.BlockSpec((1, rows, tn), lambda l, j: (l, 0, j)),
        out_shape=jax.ShapeDtypeStruct((depth, rows, n), F32),
        compiler_params=_cparams(2), name="ada_mod",
    )(c_all, ada_w, ada_b.reshape(depth, 1, n))


def _premix_kernel(*refs, tiles_per_seq, per_token, n_sub):
    ts = refs[0].shape[0] // n_sub
    n_in, n_out = 22, 9
    row_tiled = [0] + ([1, 2] if per_token else [3, 4]) + list(range(n_in, n_in + n_out))
    for sub in range(n_sub):
        win = [r.at[pl.ds(sub * ts, ts)] if j in row_tiled else r for j, r in enumerate(refs)]
        _premix_rows(*win, tiles_per_seq=tiles_per_seq, per_token=per_token, first=(sub == 0))


def _premix_rows(x_ref, sh_ref, sc_ref, cos_ref, sin_ref, gat_ref, win_ref, gql_ref, wuq_ref, gqp_ref, gkv_ref,
                 wuk_ref, wuv_ref, gkr_ref, gfq_ref, gfk_ref, fb_ref, wa2_ref, ba_ref, epl_ref, oneq_ref, onek_ref,
                 qall_ref, kall_ref, vall_ref, ckv_ref, kr_ref, fk_ref, fv_ref, lf_ref, gla_ref, carry_ref,
                 *, tiles_per_seq, per_token, first):
    i = pl.program_id(0)
    tm = x_ref.shape[0]
    x = x_ref[...]
    sh, sc = (sh_ref[...], sc_ref[...]) if per_token else (sh_ref[0], sc_ref[0])
    h = _rms(x, gat_ref[...]) * (1.0 + sc) + sh
    z = _dot(h.astype(BF16), win_ref[...])

    lane = lax.broadcasted_iota(I32, (tm, LANES), 1)
    cos, sin = cos_ref[...], sin_ref[...]

    def rope(v):
        partner = jnp.where(lane < 80, pltpu.roll(v, 112, 1), pltpu.roll(v, 16, 1))
        return v * cos + partner * sin

    small = z[:, O_SM:O_SM + LANES]

    gate = _dot(small.astype(BF16), wa2_ref[...]) + ba_ref[...]
    gla_ref[:, 0:128] = z[:, O_GQ:O_GQ + 128] * (GLA_DK ** -0.5)
    gla_ref[:, 128:256] = z[:, O_GK:O_GK + 128]
    gla_ref[:, 256:384] = _logsig(gate) * (1.0 / GLA_TAU)
    gla_ref[:, 384:640] = z[:, O_GV:O_GV + 256]
    gla_ref[:, 640:896] = z[:, O_GR:O_GR + 256]

    ckvn = _rms(z[:, O_CKV:O_CKV + MLA_KV_RANK], gkv_ref[...])
    ckv_ref[...] = ckvn
    ckvb = ckvn.astype(BF16)
    knope = _dot(ckvb, wuk_ref[...])
    one64 = jnp.where(lane == MLA_V, 1.0, 0.0)
    vm = _dot(ckvb, wuv_ref[...])
    for hd in range(MLA_HEADS):
        vall_ref[:, hd * LANES:(hd + 1) * LANES] = (vm[:, hd * LANES:(hd + 1) * LANES] + one64).astype(BF16)
    krm = jnp.where((lane >= SM_KR) & (lane < SM_KR + MLA_ROPE), small, 0.0)
    krn = krm * lax.rsqrt(jnp.sum(krm * krm, axis=-1, keepdims=True) * (1.0 / MLA_ROPE) + EPS) * gkr_ref[...]
    kro = rope(krn)
    kr_ref[...] = kro[:, SM_KR:SM_KR + MLA_ROPE]

    q = _dot(_rms(z[:, O_CQ:O_CQ + MLA_Q_RANK], gql_ref[...]).astype(BF16), wuq_ref[...])
    for hd in range(MLA_HEADS):
        sl = slice(hd * LANES, (hd + 1) * LANES)
        qh = q[:, sl]
        ms = jnp.sum(qh * qh, axis=-1, keepdims=True) * (1.0 / MLA_QK)
        qh = rope(qh * lax.rsqrt(ms + EPS) * gqp_ref[...])
        qall_ref[:, sl] = (qh * (MLA_SCALE * LOG2E)).astype(BF16)
        kall_ref[:, sl] = (knope[:, sl] + kro).astype(BF16)

    kn = []
    for n in range(FOX_KV_HEADS):
        blk = z[:, O_FK + n * LANES:O_FK + (n + 1) * LANES]
        ms = jnp.sum(blk * blk, axis=-1, keepdims=True) * (1.0 / LANES)
        kn.append(blk * lax.rsqrt(ms + EPS) * gfk_ref[...])
    fk_ref[...] = jnp.where(lane < FOX_DH, kn[0], kn[1])
    fvd = z[:, O_FV:O_FV + 256]
    fv_ref[...] = jnp.where(lane < FOX_DH, fvd[:, 0:128], fvd[:, 128:256])
    for hd in range(FOX_HEADS):
        n = hd // FOX_GROUP
        vall_ref[:, (MLA_HEADS + hd) * LANES:(MLA_HEADS + hd + 1) * LANES] = jnp.where(
            lane < FOX_DH, fvd[:, n * LANES:(n + 1) * LANES], one64).astype(BF16)

    lfv = jnp.where((lane >= SM_FF) & (lane < SM_FF + FOX_HEADS), _logsig(small + fb_ref[...]), 0.0)
    lf_ref[...] = lfv[:, SM_FF:SM_FF + FOX_HEADS]

    if per_token:
        augq = augk = None
    else:
        if first:
            @pl.when(i % tiles_per_seq == 0)
            def _():
                carry_ref[...] = jnp.zeros_like(carry_ref)
        row = lax.broadcasted_iota(I32, (tm, tm), 0)
        col = lax.broadcasted_iota(I32, (tm, tm), 1)
        tril = jnp.where(row >= col, 1.0, 0.0).astype(BF16)
        cums = _dot(tril, jnp.concatenate(_split(lfv, 3), axis=1))
        fcum = cums[:, 0:128] + cums[:, 128:256] + cums[:, 256:384] + carry_ref[0:1, :]
        carry_ref[0:1, :] = fcum[tm - 1:tm, :]
        aug = _dot(jnp.concatenate(_split(fcum * LOG2E, 3), axis=1), epl_ref[...])
        augq = aug[:, 0:512] + oneq_ref[...]
        augk = aug[:, 512:1024] + onek_ref[...]

    for hd in range(FOX_HEADS):
        blk = z[:, O_FQ + hd * LANES:O_FQ + (hd + 1) * LANES]
        ms = jnp.sum(blk * blk, axis=-1, keepdims=True) * (1.0 / FOX_DH)
        qh = blk * lax.rsqrt(ms + EPS) * gfq_ref[...] * (FOX_SCALE * LOG2E)
        kh = jnp.where(lane < FOX_DH, kn[hd // FOX_GROUP], 0.0)
        if augq is not None:
            qh = qh + augq[:, hd * LANES:(hd + 1) * LANES]
            kh = kh + augk[:, hd * LANES:(hd + 1) * LANES]
        sl = slice((MLA_HEADS + hd) * LANES, (MLA_HEADS + hd + 1) * LANES)
        qall_ref[:, sl] = qh.astype(BF16)
        kall_ref[:, sl] = kh.astype(BF16)


def _rope_tables(pos):
    half = MLA_ROPE // 2
    inv = ROPE_THETA ** (-jnp.arange(half, dtype=F32) / half)
    ang = pos.astype(F32)[:, None] * inv[None, :]
    c, s = jnp.cos(ang), jnp.sin(ang)
    n = pos.shape[0]
    cos_t = jnp.concatenate([jnp.ones((n, 64), F32), c, c, jnp.zeros((n, 32), F32)], axis=1)
    sin_t = jnp.concatenate([jnp.zeros((n, 64), F32), -s, s, jnp.zeros((n, 32), F32)], axis=1)
    return cos_t, sin_t


def _lane_vec(pieces):
    v = jnp.zeros((LANES,), F32)
    for off, val in pieces:
        v = lax.dynamic_update_slice(v, val.astype(F32), (off,))
    return v[None, :]


def _aug_constants():
    e = np.zeros((3 * LANES, 2 * FOX_HEADS * LANES), np.float32)
    oneq = np.zeros((1, FOX_HEADS * LANES), np.float32)
    onek = np.zeros((1, FOX_HEADS * LANES), np.float32)
    for hd in range(FOX_HEADS):
        for j in range(3):
            e[j * LANES + SM_FF + hd, hd * LANES + FOX_DH + j] = 1.0
            e[j * LANES + SM_FF + hd, FOX_HEADS * LANES + hd * LANES + FOX_DH + 3 + j] = -1.0
            oneq[0, hd * LANES + FOX_DH + 3 + j] = 1.0
            onek[0, hd * LANES + FOX_DH + j] = 1.0
    return jnp.asarray(e, BF16), jnp.asarray(oneq), jnp.asarray(onek)


def _layer_weights(layer, w_in, gla_wa2, gla_ba, gla_onorm, mla_qlat_norm, mla_wuq, mla_q_norm, mla_kv_norm,
                   mla_kr_norm, mla_wuk, mla_wuv, fox_q_norm, fox_k_norm, fox_fb, attn_norm):
    w = w_in[layer]
    offs = np.concatenate([[0], np.cumsum(IN_SIZES)])
    gq, gk, gv, gg, gr, cq, ckv, kr, fq, fk, fv, ff = [w[:, offs[j]:offs[j + 1]] for j in range(12)]
    z64 = jnp.zeros((D_MODEL, 64), F32)
    fq_p = jnp.concatenate([t for hd in range(FOX_HEADS) for t in (fq[:, hd * 64:(hd + 1) * 64], z64)], axis=1)
    fk_p = jnp.concatenate([fk[:, 0:64], fk[:, 0:64], fk[:, 64:128], fk[:, 64:128]], axis=1)
    fv_p = jnp.concatenate([fv[:, 0:64], fv[:, 0:64], fv[:, 64:128], fv[:, 64:128]], axis=1)
    sm = jnp.concatenate([gg, ff, jnp.zeros((D_MODEL, SM_KR - SM_FF - FOX_HEADS), F32), kr,
                          jnp.zeros((D_MODEL, LANES - SM_KR - MLA_ROPE), F32)], axis=1)
    win_p = jnp.concatenate([gq, gk, gv, gr, cq, ckv, fq_p, fk_p, fv_p, sm], axis=1).astype(BF16)

    wuq = mla_wuq[layer].reshape(MLA_Q_RANK, MLA_HEADS, MLA_QK)
    wuq_p = jnp.concatenate([wuq, jnp.zeros((MLA_Q_RANK, MLA_HEADS, LANES - MLA_QK), F32)], axis=2)
    wuq_p = wuq_p.reshape(MLA_Q_RANK, MLA_HEADS * LANES).astype(BF16)
    wuk = mla_wuk[layer]
    wuk_p = jnp.concatenate([wuk, jnp.zeros_like(wuk)], axis=2).reshape(MLA_KV_RANK, MLA_HEADS * LANES).astype(BF16)
    wuv = mla_wuv[layer]
    wuv_p = jnp.concatenate([wuv, jnp.zeros_like(wuv)], axis=2).reshape(MLA_KV_RANK, MLA_HEADS * LANES).astype(BF16)
    wa2_p = jnp.zeros((LANES, LANES), F32).at[SM_GG:SM_GG + GLA_RANK].set(gla_wa2[layer]).astype(BF16)
    return dict(
        win=win_p, wuq=wuq_p, wuk=wuk_p, wuv=wuv_p, wa2=wa2_p,
        gat=attn_norm[layer][None, :], gql=mla_qlat_norm[layer][None, :], gkv=mla_kv_norm[layer][None, :],
        gqp=_lane_vec([(0, mla_q_norm[layer])]),
        gkr=_lane_vec([(SM_KR, mla_kr_norm[layer])]),
        gfq=_lane_vec([(0, fox_q_norm[layer])]),
        gfk=_lane_vec([(0, fox_k_norm[layer]), (FOX_DH, fox_k_norm[layer])]),
        fb=_lane_vec([(SM_FF, fox_fb[layer])]),
        ba=gla_ba[layer][None, :],
        onorm=jnp.tile(gla_onorm[layer], GLA_HEADS)[None, :],
        wuk_raw=wuk, wuv_raw=mla_wuv[layer],
    )


def _premix_call(x2d, shift, scale, cos_t, sin_t, lw, aug, *, tm, tiles_per_seq, per_token):
    t = x2d.shape[0]
    nt = t // tm
    epl, oneq, onek = aug
    full = lambda a: pl.BlockSpec(a.shape, lambda i: (0,) * a.ndim)
    tok = lambda w: pl.BlockSpec((tm, w), lambda i: (i, 0))
    if per_token:
        mod_spec = tok(D_MODEL)
        pos_spec = full(cos_t)
    else:
        mod_spec = pl.BlockSpec((1, 1, D_MODEL), lambda i: (i // tiles_per_seq, 0, 0))
        pos_spec = pl.BlockSpec((tm, LANES), lambda i: (i % tiles_per_seq, 0))
    consts = [lw["gat"], lw["win"], lw["gql"], lw["wuq"], lw["gqp"], lw["gkv"], lw["wuk"], lw["wuv"], lw["gkr"],
              lw["gfq"], lw["gfk"], lw["fb"], lw["wa2"], lw["ba"], epl, oneq, onek]
    out_widths = [(ATT_HEADS * LANES, BF16), (ATT_HEADS * LANES, BF16), (ATT_HEADS * LANES, BF16), (MLA_KV_RANK, F32),
                  (MLA_ROPE, F32), (LANES, F32), (LANES, F32), (FOX_HEADS, F32), (GLA_PACK, F32)]
    return pl.pallas_call(
        functools.partial(_premix_kernel, tiles_per_seq=tiles_per_seq, per_token=per_token,
                          n_sub=2 if tm % 256 == 0 else 1),
        grid=(nt,),
        in_specs=[tok(D_MODEL), mod_spec, mod_spec, pos_spec, pos_spec] + [full(c) for c in consts],
        out_specs=[tok(w) for w, _ in out_widths],
        out_shape=[jax.ShapeDtypeStruct((t, w), dt) for w, dt in out_widths],
        scratch_shapes=[pltpu.VMEM((8, LANES), F32)],
        compiler_params=_cparams(1), name="premix",
    )(x2d, shift, scale, cos_t, sin_t, *consts)


def _flash_kernel(q_ref, k_ref, v_ref, o_ref, *, tq):
    qi = pl.program_id(2)
    row = lax.broadcasted_iota(I32, (tq, tq), 0)
    col = lax.broadcasted_iota(I32, (tq, tq), 1)
    qs = [q_ref[0, :, hh * LANES:(hh + 1) * LANES] for hh in range(2)]

    def step(kb, carry, diagonal):
        start = pl.multiple_of(kb * tq, tq)
        out = []
        for hh in range(2):
            m, acc = carry[2 * hh], carry[2 * hh + 1]
            k = k_ref[0, pl.ds(start, tq), hh * LANES:(hh + 1) * LANES]
            v = v_ref[0, pl.ds(start, tq), hh * LANES:(hh + 1) * LANES]
            s = _dot_nt(qs[hh], k)
            if diagonal:
                s = jnp.where(row >= col, s, -jnp.inf)
            m_new = jnp.maximum(m, jnp.max(s, axis=-1, keepdims=True))
            p = jnp.exp2(s - m_new)
            out += [m_new, jnp.exp2(m - m_new) * acc + _dot(p.astype(BF16), v)]
        return tuple(out)

    init = (jnp.full((tq, 1), -jnp.inf, F32), jnp.zeros((tq, LANES), F32)) * 2
    carry = lax.fori_loop(0, qi, lambda kb, c: step(kb, c, False), init)
    carry = step(qi, carry, True)
    for hh in range(2):
        acc = carry[2 * hh + 1]
        o_ref[0, :, hh * 64:(hh + 1) * 64] = (acc[:, 0:MLA_V] / acc[:, MLA_V:MLA_V + 1]).astype(BF16)


def _flash_call(q_all, k_all, v_all, *, tq):
    b, s, _ = q_all.shape
    return pl.pallas_call(
        functools.partial(_flash_kernel, tq=tq),
        grid=(b, ATT_HEADS // 2, s // tq),
        in_specs=[pl.BlockSpec((1, tq, 2 * LANES), lambda bi, p, qi: (bi, qi, p)),
                  pl.BlockSpec((1, s, 2 * LANES), lambda bi, p, qi: (bi, 0, p)),
                  pl.BlockSpec((1, s, 2 * LANES), lambda bi, p, qi: (bi, 0, p))],
        out_specs=pl.BlockSpec((1, tq, LANES), lambda bi, p, qi: (bi, qi, p)),
        out_shape=jax.ShapeDtypeStruct((b, s, ATT_HEADS * 64), BF16),
        compiler_params=_cparams(3), name="flash_attn",
    )(q_all, k_all, v_all)


def _gla_masks():
    r128 = lax.broadcasted_iota(I32, (GLA_HEADS * GLA_DK, GLA_HEADS * GLA_DV), 0)
    c256 = lax.broadcasted_iota(I32, (GLA_HEADS * GLA_DK, GLA_HEADS * GLA_DV), 1)
    return (r128 // GLA_DK) == (c256 // GLA_DV)


def _gla_out_norm(o, r, onorm):
    rr = lax.broadcasted_iota(I32, (256, 256), 0)
    cc = lax.broadcasted_iota(I32, (256, 256), 1)
    same = jnp.where((rr // GLA_DV) == (cc // GLA_DV), 1.0, 0.0).astype(BF16)
    ms = _dot((o * o).astype(BF16), same) * (1.0 / GLA_DV)
    return o * lax.rsqrt(ms + EPS) * onorm * _silu(r)


def _gla_kernel(g_ref, onorm_ref, o_ref, st_ref, sbd_ref, *, n_chunks, n_seq):
    si = pl.program_id(1)

    @pl.when(si == 0)
    def _():
        sbd_ref[...] = jnp.zeros_like(sbd_ref)

    c = GLA_CHUNK
    lane128 = lax.broadcasted_iota(I32, (1, 128), 1)
    lane256 = lax.broadcasted_iota(I32, (1, 256), 1)
    bd = _gla_masks()
    rr = lax.broadcasted_iota(I32, (c, c), 0)
    cc = lax.broadcasted_iota(I32, (c, c), 1)
    tril = jnp.where(rr >= cc, 1.0, 0.0).astype(BF16)
    r4 = lax.broadcasted_iota(I32, (GLA_HEADS * c, c), 0)
    c4 = lax.broadcasted_iota(I32, (GLA_HEADS * c, c), 1)
    causal4 = (r4 % c) >= c4
    e_r = lax.broadcasted_iota(I32, (128, 128), 0)
    e_c = lax.broadcasted_iota(I32, (128, 128), 1)
    eye = e_r == e_c

    for ci, bb in [(ci, bb) for ci in range(n_chunks) for bb in range(n_seq)]:
        blk = g_ref[bb, ci * c:(ci + 1) * c, :]
        q, k, la = blk[:, 0:128], blk[:, 128:256], blk[:, 256:384]
        v, r = blk[:, 384:640], blk[:, 640:896]
        b = _dot_split(tril, la, 2)
        b_end = b[c - 1:c, :]
        q_dec = q * jnp.exp(b)
        k_inv = (k * jnp.exp(-b)).astype(BF16)
        k_dec = (k * jnp.exp(b_end - b)).astype(BF16)
        vb = v.astype(BF16)
        s_bd = sbd_ref[bb]
        o = _dot(q_dec.astype(BF16), s_bd.astype(BF16))
        q4 = jnp.concatenate([jnp.where((lane128 // GLA_DK) == hd, q_dec, 0.0) for hd in range(GLA_HEADS)], axis=0)
        att = jnp.where(causal4, _dot_nt(q4.astype(BF16), k_inv), 0.0)
        r_all = _dot(att.astype(BF16), vb)
        for hd in range(GLA_HEADS):
            o = o + jnp.where((lane256 // GLA_DV) == hd, r_all[hd * c:(hd + 1) * c, :], 0.0)
        o_ref[bb, ci * c:(ci + 1) * c, :] = _gla_out_norm(o, r, onorm_ref[...]).astype(BF16)
        dcol = jnp.sum(jnp.where(eye, jnp.broadcast_to(jnp.exp(b_end), (128, 128)), 0.0), axis=1, keepdims=True)
        sbd_ref[bb] = s_bd * dcol + jnp.where(bd, _dot_tn(k_dec, vb), 0.0)

    @pl.when(si == pl.num_programs(1) - 1)
    def _():
        for bb in range(n_seq):
            s_bd = sbd_ref[bb]
            for hd in range(GLA_HEADS):
                st_ref[bb, hd] = s_bd[hd * GLA_DK:(hd + 1) * GLA_DK, hd * GLA_DV:(hd + 1) * GLA_DV]


def _gla_call(gla_in, onorm, *, sup):
    b, s, _ = gla_in.shape
    n_seq = 2 if b % 2 == 0 else 1
    return pl.pallas_call(
        functools.partial(_gla_kernel, n_chunks=sup // GLA_CHUNK, n_seq=n_seq),
        grid=(b // n_seq, s // sup),
        in_specs=[pl.BlockSpec((n_seq, sup, GLA_PACK), lambda bi, si: (bi, si, 0)),
                  pl.BlockSpec((1, 256), lambda bi, si: (0, 0))],
        out_specs=[pl.BlockSpec((n_seq, sup, 256), lambda bi, si: (bi, si, 0)),
                   pl.BlockSpec((n_seq, GLA_HEADS, GLA_DK, GLA_DV), lambda bi, si: (bi, 0, 0, 0))],
        out_shape=[jax.ShapeDtypeStruct((b, s, 256), BF16),
                   jax.ShapeDtypeStruct((b, GLA_HEADS, GLA_DK, GLA_DV), F32)],
        scratch_shapes=[pltpu.VMEM((n_seq, GLA_HEADS * GLA_DK, GLA_HEADS * GLA_DV), F32)],
        compiler_params=_cparams(2), name="gla_chunked",
    )(gla_in, onorm)


def _gla_step_kernel(g_ref, s0_ref, onorm_ref, o_ref, st_ref, *, nb):
    lane128 = lax.broadcasted_iota(I32, (1, 128), 1)
    lane256 = lax.broadcasted_iota(I32, (1, 256), 1)
    bd = _gla_masks()
    e_r = lax.broadcasted_iota(I32, (128, 128), 0)
    e_c = lax.broadcasted_iota(I32, (128, 128), 1)
    eye = e_r == e_c
    rows = []
    for j in range(nb):
        blk = g_ref[j:j + 1, :]
        q, k, la = blk[:, 0:128], blk[:, 128:256], blk[:, 256:384]
        v = blk[:, 384:640]
        s0 = s0_ref[j]
        s_st = s0.reshape(GLA_HEADS * GLA_DK, GLA_DV)
        s_bd = jnp.where(bd, jnp.concatenate([s_st] * GLA_HEADS, axis=1), 0.0)
        ea = jnp.exp(la)
        o = _dot((q * ea).astype(BF16), s_bd.astype(BF16))
        qk = q * k
        for hd in range(GLA_HEADS):
            dot_h = jnp.sum(jnp.where((lane128 // GLA_DK) == hd, qk, 0.0), axis=-1, keepdims=True)
            o = o + jnp.where((lane256 // GLA_DV) == hd, dot_h * v, 0.0)
        rows.append(o)
        ecol = jnp.sum(jnp.where(eye, jnp.broadcast_to(ea, (128, 128)), 0.0), axis=1, keepdims=True)
        kcol = jnp.sum(jnp.where(eye, jnp.broadcast_to(k, (128, 128)), 0.0), axis=1, keepdims=True)
        s_new = s_bd * ecol + jnp.where(bd, kcol * v, 0.0)
        for hd in range(GLA_HEADS):
            st_ref[j, hd] = s_new[hd * GLA_DK:(hd + 1) * GLA_DK, hd * GLA_DV:(hd + 1) * GLA_DV]
    o_all = jnp.concatenate(rows, axis=0)
    o_ref[...] = _gla_out_norm(o_all, g_ref[:, 640:896], onorm_ref[...]).astype(BF16)


def _gla_step_call(gla_in, state, onorm):
    t = gla_in.shape[0]
    nb = 8
    return pl.pallas_call(
        functools.partial(_gla_step_kernel, nb=nb),
        grid=(t // nb,),
        in_specs=[pl.BlockSpec((nb, GLA_PACK), lambda i: (i, 0)),
                  pl.BlockSpec((nb, GLA_HEADS, GLA_DK, GLA_DV), lambda i: (i, 0, 0, 0)),
                  pl.BlockSpec((1, 256), lambda i: (0, 0))],
        out_specs=[pl.BlockSpec((nb, 256), lambda i: (i, 0)),
                   pl.BlockSpec((nb, GLA_HEADS, GLA_DK, GLA_DV), lambda i: (i, 0, 0, 0))],
        out_shape=[jax.ShapeDtypeStruct((t, 256), BF16),
                   jax.ShapeDtypeStruct((t, GLA_HEADS, GLA_DK, GLA_DV), F32)],
        compiler_params=_cparams(1), name="gla_step",
    )(gla_in, state, onorm)


def _post_kernel(og_ref, oa_ref, wout_ref, x_ref, gm_ref, gffn_ref, sh_ref, sc_ref, rw_ref, rb_ref,
                 xmid_ref, t_ref, ew_ref, pos_ref, cnt_ref, *, per_token):
    tm = x_ref.shape[0]
    gm, sh, sc = (gm_ref[...], sh_ref[...], sc_ref[...]) if per_token else (gm_ref[0], sh_ref[0], sc_ref[0])
    mix = _dot(og_ref[...], wout_ref[0:256, :]) + _dot(oa_ref[...], wout_ref[256:1024, :])
    xm = x_ref[...] + gm * mix
    xmid_ref[...] = xm
    t = _rms(xm, gffn_ref[...]) * (1.0 + sc) + sh
    t_ref[...] = t.astype(BF16)

    t_hi, t_lo = _split(t, 2)
    logits = _dot(t_hi, rw_ref[0]) + _dot(t_hi, rw_ref[1]) + _dot(t_lo, rw_ref[0])
    lt = jnp.transpose(logits)[0:N_EXPERTS, :]
    scores = 1.0 / (1.0 + jnp.exp(-lt))
    biased = scores + rb_ref[...]
    sc_rows = [scores[e:e + 1, :] for e in range(N_EXPERTS)]
    b_rows = [biased[e:e + 1, :] for e in range(N_EXPERTS)]

    def first_max(vals):
        m = vals[0]
        for a in vals[1:]:
            m = jnp.maximum(m, a)
        idx = jnp.full(m.shape, len(vals) - 1, I32)
        for j in range(len(vals) - 2, -1, -1):
            idx = jnp.where(vals[j] == m, j, idx)
        return m, idx

    gsum, loc1, loc2 = [], [], []
    for g in range(N_GROUPS):
        a = b_rows[g * EXPERTS_PER_GROUP:(g + 1) * EXPERTS_PER_GROUP]
        m1, i1 = first_max(a)
        a2 = [jnp.where(i1 == j, -jnp.inf, a[j]) for j in range(EXPERTS_PER_GROUP)]
        m2, i2 = first_max(a2)
        gsum.append(m1 + m2)
        loc1.append(i1)
        loc2.append(i2)
    _, gsel = first_max(gsum)
    l1, l2 = loc1[N_GROUPS - 1], loc2[N_GROUPS - 1]
    for g in range(N_GROUPS - 2, -1, -1):
        l1 = jnp.where(gsel == g, loc1[g], l1)
        l2 = jnp.where(gsel == g, loc2[g], l2)
    e1 = gsel * EXPERTS_PER_GROUP + l1
    e2 = gsel * EXPERTS_PER_GROUP + l2
    w1 = jnp.zeros_like(sc_rows[0])
    w2 = jnp.zeros_like(sc_rows[0])
    for e in range(N_EXPERTS):
        w1 = jnp.where(e1 == e, sc_rows[e], w1)
        w2 = jnp.where(e2 == e, sc_rows[e], w2)
    den = w1 + w2
    ew_ref[0:1, :] = w1 / den
    ew_ref[1:2, :] = w2 / den

    sub = lax.broadcasted_iota(I32, (N_EXPERTS, tm), 0)
    rr = lax.broadcasted_iota(I32, (tm, tm), 0)
    cc = lax.broadcasted_iota(I32, (tm, tm), 1)
    triu = jnp.where(rr <= cc, 1.0, 0.0).astype(BF16)
    oh1 = jnp.where(sub == e1, 1.0, 0.0)
    oh2 = jnp.where(sub == e2, 1.0, 0.0)
    cum1 = _dot(oh1.astype(BF16), triu)
    cum2 = _dot(oh2.astype(BF16), triu)
    tot1 = cum1[:, tm - 1:tm]
    n8 = jnp.floor((tot1 + cum2[:, tm - 1:tm] + 7.0) * 0.125) * 8.0
    n8b = jnp.broadcast_to(n8, (N_EXPERTS, LANES))
    er = lax.broadcasted_iota(I32, (N_EXPERTS, N_EXPERTS), 0)
    ec = lax.broadcasted_iota(I32, (N_EXPERTS, N_EXPERTS), 1)
    off = _dot_split(jnp.where(ec < er, 1.0, 0.0).astype(BF16), n8b, 2)[:, 0:1]
    pos1 = jnp.sum(oh1 * (off + cum1 - oh1), axis=0, keepdims=True)
    pos2 = jnp.sum(oh2 * (off + tot1 + cum2 - oh2), axis=0, keepdims=True)
    pos_ref[0:1, :] = pos1.astype(I32)
    pos_ref[1:2, :] = pos2.astype(I32)
    cnt_ref[0] = n8b


def _post_call(o_gla, o_attn, w_out, x2d, gate, gffn, shift, scale, rw_pieces, rb_col, *, tm, tiles_per_seq, per_token):
    t = x2d.shape[0]
    nt = t // tm
    full = lambda a: pl.BlockSpec(a.shape, lambda i: (0,) * a.ndim)
    tok = lambda w: pl.BlockSpec((tm, w), lambda i: (i, 0))
    mod_spec = tok(D_MODEL) if per_token else pl.BlockSpec((1, 1, D_MODEL), lambda i: (i // tiles_per_seq, 0, 0))
    rowblk = pl.BlockSpec((2, tm), lambda i: (0, i))
    return pl.pallas_call(
        functools.partial(_post_kernel, per_token=per_token),
        grid=(nt,),
        in_specs=[tok(256), tok(768), full(w_out), tok(D_MODEL), mod_spec, full(gffn), mod_spec, mod_spec,
                  full(rw_pieces), full(rb_col)],
        out_specs=[tok(D_MODEL), tok(D_MODEL), rowblk, rowblk,
                   pl.BlockSpec((1, N_EXPERTS, LANES), lambda i: (i, 0, 0))],
        out_shape=[jax.ShapeDtypeStruct((t, D_MODEL), F32), jax.ShapeDtypeStruct((t, D_MODEL), BF16),
                   jax.ShapeDtypeStruct((2, t), F32), jax.ShapeDtypeStruct((2, t), I32),
                   jax.ShapeDtypeStruct((nt, N_EXPERTS, LANES), F32)],
        compiler_params=_cparams(1), name="post_mix_route",
    )(o_gla, o_attn, w_out, x2d, gate, gffn, shift, scale, rw_pieces, rb_col)


SEG_ALIGN = 8


def _local_rows(tm):
    return 2 * tm + N_EXPERTS * SEG_ALIGN


def _segment_copies(segn_ref, sego_ref, tile, local, sem, *, remote, tm, to_remote, wait):
    nbits = (2 * tm // SEG_ALIGN).bit_length()
    lo = 0
    for e in range(N_EXPERTS):
        n = segn_ref[tile * N_EXPERTS + e]
        g = sego_ref[tile * N_EXPERTS + e]
        for b in reversed(range(nbits)):
            size = SEG_ALIGN << b
            above = (n >> (b + 4)) << (b + 4)

            @pl.when((n & size) != 0)
            def _(lo=lo, g=g, above=above, size=size):
                lrows = local.at[pl.ds(pl.multiple_of(lo + above, SEG_ALIGN), size)]
                rrows = remote.at[pl.ds(pl.multiple_of(g + above, SEG_ALIGN), size)]
                cp = pltpu.make_async_copy(lrows, rrows, sem) if to_remote else pltpu.make_async_copy(rrows, lrows, sem)
                if wait:
                    cp.wait()
                else:
                    cp.start()
        lo = lo + n


def _dispatch_kernel(segn_ref, sego_ref, pos_ref, t_ref, xs_in_ref, xs_ref, stage, sem, *, tm):
    del xs_in_ref
    i = pl.program_id(0)
    n = pl.num_programs(0)
    slot = i % 2
    seg = functools.partial(_segment_copies, segn_ref, sego_ref, remote=xs_ref, tm=tm, to_remote=True)

    @pl.when(i > 0)
    def _():
        seg(i - 1, stage.at[1 - slot], sem.at[1 - slot], wait=True)

    nsl = _local_rows(tm)
    row = lax.broadcasted_iota(I32, (nsl, tm), 0)
    perm = jnp.where(row == pos_ref[0:1, :], 1.0, jnp.where(row == pos_ref[1:2, :], 1.0, 0.0)).astype(BF16)
    stage[slot] = _pack_rows(_dot(perm, t_ref[...]))
    seg(i, stage.at[slot], sem.at[slot], wait=False)

    @pl.when(i == n - 1)
    def _():
        seg(i, stage.at[slot], sem.at[slot], wait=True)


def _dispatch_call(seg_n, seg_off, pos, tb, n_slots, *, tm):
    t = tb.shape[0]
    zeros = jnp.zeros((n_slots, 512), U32)
    grid_spec = pltpu.PrefetchScalarGridSpec(
        num_scalar_prefetch=2, grid=(t // tm,),
        in_specs=[pl.BlockSpec((2, tm), lambda i, sn, so: (0, i)),
                  pl.BlockSpec((tm, D_MODEL), lambda i, sn, so: (i, 0)),
                  pl.BlockSpec(memory_space=pl.ANY)],
        out_specs=pl.BlockSpec(memory_space=pl.ANY),
        scratch_shapes=[pltpu.VMEM((2, _local_rows(tm), 512), U32), pltpu.SemaphoreType.DMA((2,))])
    return pl.pallas_call(
        functools.partial(_dispatch_kernel, tm=tm), grid_spec=grid_spec,
        out_shape=jax.ShapeDtypeStruct((n_slots, 512), U32),
        input_output_aliases={4: 0},
        compiler_params=_cparams(1), name="moe_dispatch",
    )(seg_n, seg_off, pos, tb, zeros)


def _expert_kernel(be_ref, nu_ref, xs_ref, w1_ref, w3_ref, w2_ref, ys_ref):
    j = pl.program_id(0)

    @pl.when(j < nu_ref[0])
    def _():
        xa, xb = _unpack_rows(xs_ref[...])
        xa, xb = xa.astype(BF16), xb.astype(BF16)
        h1 = _dot(xa, w1_ref[0, 0:512, :]) + _dot(xb, w1_ref[0, 512:1024, :])
        h3 = _dot(xa, w3_ref[0, 0:512, :]) + _dot(xb, w3_ref[0, 512:1024, :])
        y = _dot((_silu(h1) * h3).astype(BF16), w2_ref[0])
        ys_ref[...] = _pack_rows(y)

    @pl.when(j >= nu_ref[0])
    def _():
        ys_ref[...] = jnp.zeros_like(ys_ref)


def _expert_call(block_expert, n_used, xs, w1, w3, w2):
    n_slots = xs.shape[0]
    nb = n_slots // MOE_BLK
    grid_spec = pltpu.PrefetchScalarGridSpec(
        num_scalar_prefetch=2, grid=(nb,),
        in_specs=[pl.BlockSpec((MOE_BLK, 512), lambda j, be, nu: (j, 0)),
                  pl.BlockSpec((1, D_MODEL, D_EXPERT), lambda j, be, nu: (be[j], 0, 0)),
                  pl.BlockSpec((1, D_MODEL, D_EXPERT), lambda j, be, nu: (be[j], 0, 0)),
                  pl.BlockSpec((1, D_EXPERT, D_MODEL), lambda j, be, nu: (be[j], 0, 0))],
        out_specs=pl.BlockSpec((MOE_BLK, 512), lambda j, be, nu: (j, 0)))
    return pl.pallas_call(
        _expert_kernel, grid_spec=grid_spec,
        out_shape=jax.ShapeDtypeStruct((n_slots, 512), U32),
        compiler_params=_cparams(1), name="moe_experts",
    )(block_expert, n_used, xs, w1, w3, w2)


def _combine_kernel(segn_ref, sego_ref, ys_ref, xmid_ref, gc_ref, wcol_ref, pcol_ref, o_ref, ybuf, sem, *, tm, per_token):
    i = pl.program_id(0)
    n = pl.num_programs(0)
    slot = i % 2
    seg = functools.partial(_segment_copies, segn_ref, sego_ref, remote=ys_ref, tm=tm, to_remote=False)

    @pl.when(i == 0)
    def _():
        ybuf[...] = jnp.zeros_like(ybuf)
        seg(0, ybuf.at[0], sem.at[0], wait=False)

    @pl.when(i + 1 < n)
    def _():
        seg(i + 1, ybuf.at[1 - slot], sem.at[1 - slot], wait=False)

    seg(i, ybuf.at[slot], sem.at[slot], wait=True)

    nsl = _local_rows(tm)
    lane = lax.broadcasted_iota(I32, (tm, nsl), 1)
    w, p = wcol_ref[...], pcol_ref[...]
    wm = jnp.where(lane == p[:, 0:1], w[:, 0:1], 0.0) + jnp.where(lane == p[:, 1:2], w[:, 1:2], 0.0)
    w_hi, w_lo = _split(wm, 2)
    ya, yb = _unpack_rows(ybuf[slot])
    ya, yb = ya.astype(BF16), yb.astype(BF16)
    gc = gc_ref[...] if per_token else gc_ref[0]
    o_ref[:, 0:512] = xmid_ref[:, 0:512] + gc[:, 0:512] * (_dot(w_hi, ya) + _dot(w_lo, ya))
    o_ref[:, 512:1024] = xmid_ref[:, 512:1024] + gc[:, 512:1024] * (_dot(w_hi, yb) + _dot(w_lo, yb))


def _combine_call(seg_n, seg_off, ys, xmid, gate, wcol, pcol, *, tm, tiles_per_seq, per_token):
    t = xmid.shape[0]
    mod_spec = (pl.BlockSpec((tm, D_MODEL), lambda i, sn, so: (i, 0)) if per_token
                else pl.BlockSpec((1, 1, D_MODEL), lambda i, sn, so: (i // tiles_per_seq, 0, 0)))
    grid_spec = pltpu.PrefetchScalarGridSpec(
        num_scalar_prefetch=2, grid=(t // tm,),
        in_specs=[pl.BlockSpec(memory_space=pl.ANY),
                  pl.BlockSpec((tm, D_MODEL), lambda i, sn, so: (i, 0)), mod_spec,
                  pl.BlockSpec((tm, 2), lambda i, sn, so: (i, 0)),
                  pl.BlockSpec((tm, 2), lambda i, sn, so: (i, 0))],
        out_specs=pl.BlockSpec((tm, D_MODEL), lambda i, sn, so: (i, 0)),
        scratch_shapes=[pltpu.VMEM((2, _local_rows(tm), 512), U32), pltpu.SemaphoreType.DMA((2,))])
    return pl.pallas_call(
        functools.partial(_combine_kernel, tm=tm, per_token=per_token), grid_spec=grid_spec,
        out_shape=jax.ShapeDtypeStruct((t, D_MODEL), F32),
        compiler_params=_cparams(1), name="moe_combine",
    )(seg_n, seg_off, ys, xmid, gate, wcol, pcol)


def _moe(tb, ew, pos, cnt, xmid, gate, w1, w3, w2, *, tm, tiles_per_seq, per_token):
    t = xmid.shape[0]
    nt = t // tm
    n8 = cnt[:, :, 0].astype(I32)
    total = jnp.sum(n8, axis=0)
    padded = (total + MOE_BLK - 1) // MOE_BLK * MOE_BLK
    pend = jnp.cumsum(padded)
    seg_off = ((pend - padded)[None, :] + jnp.cumsum(n8, axis=0) - n8).astype(I32).reshape(-1)
    seg_n = n8.reshape(-1)
    nb = -(-(2 * t + nt * N_EXPERTS * (SEG_ALIGN - 1)) // MOE_BLK) + N_EXPERTS
    block_expert = jnp.clip(jnp.searchsorted(pend, jnp.arange(nb, dtype=I32) * MOE_BLK, side="right"),
                            0, N_EXPERTS - 1).astype(I32)
    n_used = (pend[-1:] // MOE_BLK).astype(I32)
    xs = _dispatch_call(seg_n, seg_off, pos, tb, nb * MOE_BLK, tm=tm)
    ys = _expert_call(block_expert, n_used, xs, w1, w3, w2)
    return _combine_call(seg_n, seg_off, ys, xmid, gate, jnp.transpose(ew), jnp.transpose(pos), tm=tm,
                         tiles_per_seq=tiles_per_seq, per_token=per_token)


def _qprep_kernel(qall_ref, wukt_ref, ql_ref, qr_ref, qf_ref):
    for hd in range(MLA_HEADS):
        blk = qall_ref[:, hd * LANES:(hd + 1) * LANES]
        ql_ref[:, hd, :] = _dot(blk, wukt_ref[hd])
        qr_ref[:, hd, :] = blk[:, MLA_NOPE:MLA_QK].astype(F32)
    for hd in range(FOX_HEADS):
        qf_ref[:, hd, :] = qall_ref[:, (MLA_HEADS + hd) * LANES:(MLA_HEADS + hd) * LANES + FOX_DH].astype(F32)


def _qprep_call(q_all, wukt_p):
    t = q_all.shape[0]
    return pl.pallas_call(
        _qprep_kernel,
        out_shape=[jax.ShapeDtypeStruct((t, MLA_HEADS, MLA_KV_RANK), F32),
                   jax.ShapeDtypeStruct((t, MLA_HEADS, MLA_ROPE), F32),
                   jax.ShapeDtypeStruct((t, FOX_HEADS, FOX_DH), F32)],
        compiler_params=pltpu.CompilerParams(vmem_limit_bytes=VMEM_LIMIT), name="sample_qprep",
    )(q_all, wukt_p)


def _foxbias_kernel(pt_ref, lf_hbm, o_ref, buf, sem, *, layer, n_pages):
    b = pl.program_id(0)
    nb = pl.num_programs(0)

    def copies(bb, slot):
        return [pltpu.make_async_copy(lf_hbm.at[layer, pt_ref[bb, p]], buf.at[slot, p], sem.at[slot])
                for p in range(n_pages)]

    @pl.when(b == 0)
    def _():
        for cp in copies(0, 0):
            cp.start()

    slot = b % 2

    @pl.when(b + 1 < nb)
    def _():
        for cp in copies(b + 1, 1 - slot):
            cp.start()

    for cp in copies(b, slot):
        cp.wait()

    rr = lax.broadcasted_iota(I32, (PAGE_SIZE, PAGE_SIZE), 0)
    cc = lax.broadcasted_iota(I32, (PAGE_SIZE, PAGE_SIZE), 1)
    m_suf = jnp.where(rr > cc, 1.0, 0.0).astype(BF16)
    m_tot = jnp.ones((PAGE_SIZE, PAGE_SIZE), BF16)
    pr = lax.broadcasted_iota(I32, (n_pages, n_pages), 0)
    pc = lax.broadcasted_iota(I32, (n_pages, n_pages), 1)
    later = jnp.where(pc > pr, 1.0, 0.0).astype(BF16)
    for hd in range(FOX_HEADS):
        pieces = _split(buf[slot, :, hd, :], 3)
        within = sum(_dot(p, m_suf) for p in pieces)
        total = sum(_dot(p, m_tot) for p in pieces)
        o_ref[0, hd] = within + _dot_split(later, total, 3)


def _foxbias_call(page_table, lf_cache, layer):
    nb, n_pages = page_table.shape
    grid_spec = pltpu.PrefetchScalarGridSpec(
        num_scalar_prefetch=1, grid=(nb,),
        in_specs=[pl.BlockSpec(memory_space=pl.ANY)],
        out_specs=pl.BlockSpec((1, FOX_HEADS, n_pages, PAGE_SIZE), lambda b, pt: (b, 0, 0, 0)),
        scratch_shapes=[pltpu.VMEM((2, n_pages, FOX_HEADS, PAGE_SIZE), F32), pltpu.SemaphoreType.DMA((2,))])
    return pl.pallas_call(
        functools.partial(_foxbias_kernel, layer=layer, n_pages=n_pages), grid_spec=grid_spec,
        out_shape=jax.ShapeDtypeStruct((nb, FOX_HEADS, n_pages, PAGE_SIZE), F32),
        compiler_params=_cparams(1), name="fox_past_bias",
    )(page_table, lf_cache)


def _paged_kernel(pt_ref, ql_ref, qr_ref, qf_ref, bias_ref, gnew_ref, ckvn_ref, krn_ref, fkn_ref, fvn_ref,
                  ckv_hbm, kr_hbm, fk_hbm, fv_hbm, ol_ref, of_ref,
                  ckv_buf, kr_buf, fk_buf, fv_buf, sem, m_ref, l_ref, acc_ref, mf_ref, lf_ref, accf_ref,
                  *, layer, n_chunks, pg):
    s = pl.program_id(0)
    ns = pl.num_programs(0)
    chunk = s % n_chunks

    def copies(step, slot):
        bb, ch = step // n_chunks, step % n_chunks
        out = []
        for p in range(pg):
            phys = pt_ref[bb, ch * pg + p]
            keys = pl.ds(p * PAGE_SIZE, PAGE_SIZE)
            out.append(pltpu.make_async_copy(ckv_hbm.at[layer, phys], ckv_buf.at[slot, keys], sem.at[slot, 0]))
            out.append(pltpu.make_async_copy(kr_hbm.at[layer, phys], kr_buf.at[slot, :, keys], sem.at[slot, 1]))
            out.append(pltpu.make_async_copy(fk_hbm.at[layer, phys], fk_buf.at[slot, :, :, keys], sem.at[slot, 2]))
            out.append(pltpu.make_async_copy(fv_hbm.at[layer, phys], fv_buf.at[slot, :, :, keys], sem.at[slot, 3]))
        return out

    @pl.when(s == 0)
    def _():
        for cp in copies(0, 0):
            cp.start()

    slot = s % 2

    @pl.when(s + 1 < ns)
    def _():
        for cp in copies(s + 1, 1 - slot):
            cp.start()

    @pl.when(chunk == 0)
    def _():
        m_ref[...] = jnp.full_like(m_ref, -1e30)
        l_ref[...] = jnp.zeros_like(l_ref)
        acc_ref[...] = jnp.zeros_like(acc_ref)
        mf_ref[...] = jnp.full_like(mf_ref, -1e30)
        lf_ref[...] = jnp.zeros_like(lf_ref)
        accf_ref[...] = jnp.zeros_like(accf_ref)

    for cp in copies(s, slot):
        cp.wait()

    def merge(m_r, l_r, acc_r, sc, pv):
        m_old = m_r[:, 0:1]
        m_new = jnp.maximum(m_old, jnp.max(sc, axis=-1, keepdims=True))
        alpha = jnp.exp2(m_old - m_new)
        p = jnp.exp2(sc - m_new)
        l_r[...] = jnp.broadcast_to(alpha * l_r[:, 0:1] + jnp.sum(p, axis=-1, keepdims=True), l_r.shape)
        acc_r[...] = alpha * acc_r[...] + pv(p)
        m_r[...] = jnp.broadcast_to(m_new, m_r.shape)

    ql, qr, qf = ql_ref[0], qr_ref[0], qf_ref[0]
    ckv = ckv_buf[slot]
    merge(m_ref, l_ref, acc_ref, _dot_nt(ql, ckv) + _dot(qr, kr_buf[slot]), lambda p: _dot(p, ckv))
    sf = jnp.concatenate([_dot(qf[n * FOX_GROUP:(n + 1) * FOX_GROUP], fk_buf[slot, n])
                          for n in range(FOX_KV_HEADS)], axis=0)
    merge(mf_ref, lf_ref, accf_ref, sf + (bias_ref[0] + gnew_ref[0]) * LOG2E,
          lambda p: jnp.concatenate([_dot_nt(p[n * FOX_GROUP:(n + 1) * FOX_GROUP], fv_buf[slot, n])
                                     for n in range(FOX_KV_HEADS)], axis=0))

    @pl.when(chunk == n_chunks - 1)
    def _():
        ckvn = ckvn_ref[0]
        s_new = jnp.sum(ql * ckvn, axis=-1, keepdims=True) + jnp.sum(qr * krn_ref[0], axis=-1, keepdims=True)
        merge(m_ref, l_ref, acc_ref, s_new, lambda p: p * ckvn)
        ol_ref[0] = acc_ref[...] / l_ref[:, 0:1]
        rep = lambda a: jnp.concatenate([a[:, n * FOX_DH:(n + 1) * FOX_DH] for n in range(FOX_KV_HEADS)
                                         for _ in range(FOX_GROUP)], axis=0)
        fkn, fvn = rep(fkn_ref[0]), rep(fvn_ref[0])
        sf_new = jnp.sum(qf * fkn, axis=-1, keepdims=True)
        merge(mf_ref, lf_ref, accf_ref, sf_new, lambda p: p * fvn)
        of_ref[0] = accf_ref[...] / lf_ref[:, 0:1]


def _paged_call(page_table, ql, qr, qf, bias, gnew, ckvn, krn, fkn, fvn, c_ckv, c_kr, c_fk, c_fv, layer, *, pg):
    nb, n_pages = page_table.shape
    n_chunks = n_pages // pg
    kc = pg * PAGE_SIZE
    per_b = lambda shp: pl.BlockSpec((1,) + shp, lambda s, pt: (s // n_chunks,) + (0,) * len(shp))
    grid_spec = pltpu.PrefetchScalarGridSpec(
        num_scalar_prefetch=1, grid=(nb * n_chunks,),
        in_specs=[per_b((MLA_HEADS, MLA_KV_RANK)), per_b((MLA_HEADS, MLA_ROPE)), per_b((FOX_HEADS, FOX_DH)),
                  pl.BlockSpec((1, FOX_HEADS, kc), lambda s, pt: (s // n_chunks, 0, s % n_chunks)),
                  per_b((FOX_HEADS, 1)), per_b((1, MLA_KV_RANK)), per_b((1, MLA_ROPE)), per_b((1, LANES)),
                  per_b((1, LANES))] + [pl.BlockSpec(memory_space=pl.ANY)] * 4,
        out_specs=[per_b((MLA_HEADS, MLA_KV_RANK)), per_b((FOX_HEADS, FOX_DH))],
        scratch_shapes=[pltpu.VMEM((2, kc, MLA_KV_RANK), F32), pltpu.VMEM((2, MLA_ROPE, kc), F32),
                        pltpu.VMEM((2, FOX_KV_HEADS, FOX_DH, kc), F32), pltpu.VMEM((2, FOX_KV_HEADS, FOX_DH, kc), F32),
                        pltpu.SemaphoreType.DMA((2, 4)),
                        pltpu.VMEM((MLA_HEADS, LANES), F32), pltpu.VMEM((MLA_HEADS, LANES), F32),
                        pltpu.VMEM((MLA_HEADS, MLA_KV_RANK), F32),
                        pltpu.VMEM((FOX_HEADS, LANES), F32), pltpu.VMEM((FOX_HEADS, LANES), F32),
                        pltpu.VMEM((FOX_HEADS, FOX_DH), F32)])
    return pl.pallas_call(
        functools.partial(_paged_kernel, layer=layer, n_chunks=n_chunks, pg=pg), grid_spec=grid_spec,
        out_shape=[jax.ShapeDtypeStruct((nb, MLA_HEADS, MLA_KV_RANK), F32),
                   jax.ShapeDtypeStruct((nb, FOX_HEADS, FOX_DH), F32)],
        compiler_params=_cparams(1), name="paged_attn",
    )(page_table, ql, qr, qf, bias, gnew, ckvn, krn, fkn, fvn, c_ckv, c_kr, c_fk, c_fv)


def _oproj_kernel(ol_ref, of_ref, wuv_ref, o_ref):
    for hd in range(MLA_HEADS):
        o_ref[:, hd * MLA_V:(hd + 1) * MLA_V] = _dot(ol_ref[:, hd, :].astype(BF16), wuv_ref[hd]).astype(BF16)
    for hd in range(FOX_HEADS):
        o_ref[:, 512 + hd * FOX_DH:512 + (hd + 1) * FOX_DH] = of_ref[:, hd, :].astype(BF16)


def _oproj_call(o_lat, o_fox, wuv_h):
    t = o_lat.shape[0]
    return pl.pallas_call(
        _oproj_kernel, out_shape=jax.ShapeDtypeStruct((t, ATT_HEADS * 64), BF16),
        compiler_params=pltpu.CompilerParams(vmem_limit_bytes=VMEM_LIMIT), name="sample_oproj",
    )(o_lat, o_fox, wuv_h)


def kernel(x_prompt, x_sample, cache_mla_ckv, cache_mla_krope, cache_fox_k, cache_fox_v, cache_fox_logf, state_gla, page_table, c_prompt, c_sample, router_w, router_b, attn_norm, ffn_norm, ada_w, ada_b, w_in, gla_wa2, gla_ba, gla_onorm, mla_qlat_norm, mla_wuq, mla_q_norm, mla_kv_norm, mla_kr_norm, mla_wuk, mla_wuv, fox_q_norm, fox_k_norm, fox_fb, w_out, moe_w1, moe_w3, moe_w2):
    bp, seq, _ = x_prompt.shape
    bs = x_sample.shape[0]
    depth = w_in.shape[0]
    n_pages = page_table.shape[1]
    tp = bp * seq
    tm_p = min(TM_PROMPT, seq)
    tps = seq // tm_p
    tm_m = min(TM_MOE, seq)
    tps_m = seq // tm_m
    tq = min(TQ, seq)
    sup = min(GLA_SUPER, seq)
    pg = min(PAGES_PER_STEP, n_pages)

    mods = _ada_call(jnp.concatenate([c_prompt, c_sample], axis=0), ada_w, ada_b)
    cos_p, sin_p = _rope_tables(jnp.arange(seq))
    cos_s, sin_s = _rope_tables(jnp.full((1,), n_pages * PAGE_SIZE))
    aug = _aug_constants()
    rw_pad = jnp.zeros((D_MODEL, LANES), F32).at[:, :N_EXPERTS].set(router_w)
    rw_hi = rw_pad.astype(BF16)
    rw_pieces = jnp.stack([rw_hi, (rw_pad - rw_hi.astype(F32)).astype(BF16)])
    rb_col = router_b.astype(F32)[:, None]
    c_kr = jnp.transpose(cache_mla_krope, (0, 1, 3, 2))
    c_fk = jnp.transpose(cache_fox_k, (0, 1, 3, 4, 2))
    c_fv = jnp.transpose(cache_fox_v, (0, 1, 3, 4, 2))
    c_lf = jnp.transpose(cache_fox_logf, (0, 1, 3, 2))

    xp = x_prompt.reshape(tp, D_MODEL)
    xs = x_sample.reshape(bs, D_MODEL)
    p_ent, s_ent = [], []
    for layer in range(depth):
        lw = _layer_weights(layer, w_in, gla_wa2, gla_ba, gla_onorm, mla_qlat_norm, mla_wuq, mla_q_norm,
                            mla_kv_norm, mla_kr_norm, mla_wuk, mla_wuv, fox_q_norm, fox_k_norm, fox_fb, attn_norm)
        wout_b = w_out[layer].astype(BF16)
        w1b, w3b, w2b = moe_w1[layer].astype(BF16), moe_w3[layer].astype(BF16), moe_w2[layer].astype(BF16)
        gffn = ffn_norm[layer][None, :]
        mod = mods[layer]
        mp = [mod[:bp, j * D_MODEL:(j + 1) * D_MODEL].reshape(bp, 1, D_MODEL) for j in range(6)]
        ms = [mod[bp:, j * D_MODEL:(j + 1) * D_MODEL] for j in range(6)]

        (q_all, k_all, v_all, ckv, kr, fk, fv, lf, gla_in) = _premix_call(
            xp, mp[0], mp[1], cos_p, sin_p, lw, aug, tm=tm_p, tiles_per_seq=tps, per_token=False)
        o_gla, st = _gla_call(gla_in.reshape(bp, seq, GLA_PACK), lw["onorm"], sup=sup)
        o_att = _flash_call(q_all.reshape(bp, seq, -1), k_all.reshape(bp, seq, -1), v_all.reshape(bp, seq, -1), tq=tq)
        p_ent.append((ckv.reshape(bp, seq, MLA_KV_RANK), kr.reshape(bp, seq, MLA_ROPE),
                      fk.reshape(bp, seq, FOX_KV_HEADS, FOX_DH), fv.reshape(bp, seq, FOX_KV_HEADS, FOX_DH),
                      lf.reshape(bp, seq, FOX_HEADS), st))
        xmid, tb, ew, pos, cnt = _post_call(
            o_gla.reshape(tp, 256), o_att.reshape(tp, -1), wout_b, xp, mp[2], gffn, mp[3], mp[4], rw_pieces, rb_col,
            tm=tm_m, tiles_per_seq=tps_m, per_token=False)
        xp = _moe(tb, ew, pos, cnt, xmid, mp[5], w1b, w3b, w2b, tm=tm_m, tiles_per_seq=tps_m, per_token=False)

        (q_all, _, _, ckv, kr, fk, fv, lf, gla_in) = _premix_call(
            xs, ms[0], ms[1], cos_s, sin_s, lw, aug, tm=bs, tiles_per_seq=1, per_token=True)
        o_gla, st = _gla_step_call(gla_in, state_gla[layer], lw["onorm"])
        wukt_p = jnp.concatenate([jnp.transpose(lw["wuk_raw"], (1, 2, 0)),
                                  jnp.zeros((MLA_HEADS, LANES - MLA_NOPE, MLA_KV_RANK), F32)], axis=1).astype(BF16)
        ql, qr, qf = _qprep_call(q_all, wukt_p)
        bias = _foxbias_call(page_table, c_lf, layer).reshape(bs, FOX_HEADS, n_pages * PAGE_SIZE)
        o_lat, o_fox = _paged_call(page_table, ql, qr, qf, bias, lf.reshape(bs, FOX_HEADS, 1),
                                   ckv.reshape(bs, 1, -1), kr.reshape(bs, 1, -1), fk.reshape(bs, 1, -1),
                                   fv.reshape(bs, 1, -1), cache_mla_ckv, c_kr, c_fk, c_fv, layer, pg=pg)
        o_att = _oproj_call(o_lat, o_fox, jnp.transpose(lw["wuv_raw"], (1, 0, 2)).astype(BF16))
        s_ent.append((ckv.reshape(bs, 1, MLA_KV_RANK), kr.reshape(bs, 1, MLA_ROPE),
                      fk.reshape(bs, 1, FOX_KV_HEADS, FOX_DH), fv.reshape(bs, 1, FOX_KV_HEADS, FOX_DH),
                      lf.reshape(bs, 1, FOX_HEADS), st))
        xmid, tb, ew, pos, cnt = _post_call(
            o_gla, o_att, wout_b, xs, ms[2], gffn, ms[3], ms[4], rw_pieces, rb_col,
            tm=bs, tiles_per_seq=1, per_token=True)
        xs = _moe(tb, ew, pos, cnt, xmid, ms[5], w1b, w3b, w2b, tm=bs, tiles_per_seq=1, per_token=True)

    stack = lambda ents, j: jnp.stack([e[j] for e in ents], axis=0)
    return (xp.reshape(bp, seq, D_MODEL), xs.reshape(bs, 1, D_MODEL),
            stack(p_ent, 0), stack(p_ent, 1), stack(p_ent, 2), stack(p_ent, 3), stack(p_ent, 4), stack(p_ent, 5),
            stack(s_ent, 0), stack(s_ent, 1), stack(s_ent, 2), stack(s_ent, 3), stack(s_ent, 4), stack(s_ent, 5))
```

```python
import functools

import numpy as np
import jax
import jax.numpy as jnp
from jax import lax
from jax.experimental import pallas as pl
from jax.experimental.pallas import tpu as pltpu

F32, BF16, I32, U32 = jnp.float32, jnp.bfloat16, jnp.int32, jnp.uint32

D_MODEL = 1024
EPS = 1e-6
GLA_HEADS, GLA_DK, GLA_DV, GLA_RANK, GLA_TAU, GLA_CHUNK = 4, 32, 64, 16, 16.0, 64
MLA_HEADS, MLA_Q_RANK, MLA_KV_RANK, MLA_NOPE, MLA_ROPE, MLA_V = 8, 384, 256, 64, 32, 64
MLA_QK = MLA_NOPE + MLA_ROPE
MLA_SCALE = MLA_QK ** -0.5
ROPE_THETA = 10000.0
FOX_HEADS, FOX_KV_HEADS, FOX_GROUP, FOX_DH = 4, 2, 2, 64
FOX_SCALE = FOX_DH ** -0.5
N_EXPERTS, N_GROUPS, EXPERTS_PER_GROUP, D_EXPERT = 16, 4, 4, 512
PAGE_SIZE = 128
IN_SIZES = (128, 128, 256, 16, 256, 384, 256, 32, 256, 128, 128, 4)
ATT_HEADS = MLA_HEADS + FOX_HEADS
LOG2E = 1.4426950408889634

LANES = 128
VMEM_LIMIT = 56 * 1024 * 1024

O_GQ, O_GK, O_GV, O_GR, O_CQ, O_CKV, O_FQ, O_FK, O_FV, O_SM = 0, 128, 256, 512, 768, 1152, 1408, 1920, 2176, 2432
W_IN_PACKED = 2560
SM_GG, SM_FF, SM_KR = 0, 16, 64
GLA_PACK = 896

TM_PROMPT = 512
TM_MOE = 256
TQ = 256
TK = 128
GLA_SUPER = 512
MOE_BLK = 512
PAGES_PER_STEP = 32


def _cparams(n_axes):
    return pltpu.CompilerParams(dimension_semantics=("arbitrary",) * n_axes, vmem_limit_bytes=VMEM_LIMIT)


def _dot(a, b):
    return jnp.dot(a, b, preferred_element_type=F32)


def _dot_nt(a, b):
    return lax.dot_general(a, b, (((1,), (1,)), ((), ())), preferred_element_type=F32)


def _dot_tn(a, b):
    return lax.dot_general(a, b, (((0,), (0,)), ((), ())), preferred_element_type=F32)


def _split(x, n):
    out, r = [], x
    for _ in range(n):
        p = r.astype(BF16)
        out.append(p)
        r = r - p.astype(F32)
    return out


def _dot_split(a_exact, b, n):
    acc = None
    for p in _split(b, n):
        t = _dot(a_exact, p)
        acc = t if acc is None else acc + t
    return acc


def _logsig(x):
    return jnp.minimum(x, 0.0) - jnp.log1p(jnp.exp(-jnp.abs(x)))


def _silu(x):
    return x / (1.0 + jnp.exp(-x))


def _rms(x, gain):
    return x * lax.rsqrt(jnp.mean(x * x, axis=-1, keepdims=True) + EPS) * gain


def _pack_rows(x):
    hi = pltpu.bitcast(x[:, :512].astype(BF16).astype(F32), U32)
    lo = pltpu.bitcast(x[:, 512:].astype(BF16).astype(F32), U32) >> 16
    return hi | lo


def _unpack_rows(w):
    return pltpu.bitcast(w & jnp.uint32(0xFFFF0000), F32), pltpu.bitcast(w << 16, F32)


def _ada_kernel(c_ref, w_ref, b_ref, o_ref):
    s_hi, s_lo = _split(_silu(c_ref[...]), 2)
    w_hi, w_lo = _split(w_ref[0], 2)
    o_ref[0] = _dot(s_hi, w_hi) + _dot(s_hi, w_lo) + _dot(s_lo, w_hi) + b_ref[0]


def _ada_call(c_all, ada_w, ada_b):
    rows, depth, n = c_all.shape[0], ada_w.shape[0], ada_w.shape[2]
    tn = n // 4
    return pl.pallas_call(
        _ada_kernel, grid=(depth, n // tn),
        in_specs=[pl.BlockSpec((rows, D_MODEL), lambda l, j: (0, 0)),
                  pl.BlockSpec((1, D_MODEL, tn), lambda l, j: (l, 0, j)),
                  pl.BlockSpec((1, 1, tn), lambda l, j: (l, 0, j))],
        out_specs=pl.BlockSpec((1, rows, tn), lambda l, j: (l, 0, j)),
        out_shape=jax.ShapeDtypeStruct((depth, rows, n), F32),
        compiler_params=_cparams(2), name="ada_mod",
    )(c_all, ada_w, ada_b.reshape(depth, 1, n))


def _premix_kernel(*refs, tiles_per_seq, per_token, n_sub):
    ts = refs[0].shape[0] // n_sub
    n_in, n_out = 22, 9
    row_tiled = [0] + ([1, 2] if per_token else [3, 4]) + [j for j in range(n_in, n_in + n_out) if j != n_in + 2]
    for sub in range(n_sub):
        win = [r.at[pl.ds(sub * ts, ts)] if j in row_tiled else r for j, r in enumerate(refs)]
        win[n_in + 2] = refs[n_in + 2].at[0, :, pl.ds(sub * ts, ts)]
        _premix_rows(*win, tiles_per_seq=tiles_per_seq, per_token=per_token, first=(sub == 0))


def _premix_rows(x_ref, sh_ref, sc_ref, cos_ref, sin_ref, gat_ref, win_ref, gql_ref, wuq_ref, gqp_ref, gkv_ref,
                 wuk_ref, wuv_ref, gkr_ref, gfq_ref, gfk_ref, fb_ref, wa2_ref, ba_ref, epl_ref, oneq_ref, onek_ref,
                 qall_ref, kall_ref, vall_ref, ckv_ref, kr_ref, fk_ref, fv_ref, lf_ref, gla_ref, carry_ref,
                 *, tiles_per_seq, per_token, first):
    i = pl.program_id(0)
    tm = x_ref.shape[0]
    x = x_ref[...]
    sh, sc = (sh_ref[...], sc_ref[...]) if per_token else (sh_ref[0], sc_ref[0])
    h = _rms(x, gat_ref[...]) * (1.0 + sc) + sh
    z = _dot(h.astype(BF16), win_ref[...])

    lane = lax.broadcasted_iota(I32, (tm, LANES), 1)
    cos, sin = cos_ref[...], sin_ref[...]

    def rope(v):
        partner = jnp.where(lane < 80, pltpu.roll(v, 112, 1), pltpu.roll(v, 16, 1))
        return v * cos + partner * sin

    small = z[:, O_SM:O_SM + LANES]

    gate = _dot(small.astype(BF16), wa2_ref[...]) + ba_ref[...]
    gla_ref[:, 0:128] = z[:, O_GQ:O_GQ + 128] * (GLA_DK ** -0.5)
    gla_ref[:, 128:256] = z[:, O_GK:O_GK + 128]
    gla_ref[:, 256:384] = _logsig(gate) * (1.0 / GLA_TAU)
    gla_ref[:, 384:640] = z[:, O_GV:O_GV + 256]
    gla_ref[:, 640:896] = z[:, O_GR:O_GR + 256]

    ckvn = _rms(z[:, O_CKV:O_CKV + MLA_KV_RANK], gkv_ref[...])
    ckv_ref[...] = ckvn
    ckvb = ckvn.astype(BF16)
    knope = _dot(ckvb, wuk_ref[...])
    one64 = jnp.where(lane == MLA_V, 1.0, 0.0)
    vm = _dot(ckvb, wuv_ref[...])
    for hd in range(MLA_HEADS):
        vall_ref[hd * LANES:(hd + 1) * LANES, :] = jnp.transpose(vm[:, hd * LANES:(hd + 1) * LANES] + one64).astype(BF16)
    krm = jnp.where((lane >= SM_KR) & (lane < SM_KR + MLA_ROPE), small, 0.0)
    krn = krm * lax.rsqrt(jnp.sum(krm * krm, axis=-1, keepdims=True) * (1.0 / MLA_ROPE) + EPS) * gkr_ref[...]
    kro = rope(krn)
    kr_ref[...] = kro[:, SM_KR:SM_KR + MLA_ROPE]

    q = _dot(_rms(z[:, O_CQ:O_CQ + MLA_Q_RANK], gql_ref[...]).astype(BF16), wuq_ref[...])
    for hd in range(MLA_HEADS):
        sl = slice(hd * LANES, (hd + 1) * LANES)
        qh = q[:, sl]
        ms = jnp.sum(qh * qh, axis=-1, keepdims=True) * (1.0 / MLA_QK)
        qh = rope(qh * lax.rsqrt(ms + EPS) * gqp_ref[...])
        qall_ref[:, sl] = (qh * (MLA_SCALE * LOG2E)).astype(BF16)
        kall_ref[:, sl] = (knope[:, sl] + kro).astype(BF16)

    kn = []
    for n in range(FOX_KV_HEADS):
        blk = z[:, O_FK + n * LANES:O_FK + (n + 1) * LANES]
        ms = jnp.sum(blk * blk, axis=-1, keepdims=True) * (1.0 / LANES)
        kn.append(blk * lax.rsqrt(ms + EPS) * gfk_ref[...])
    fk_ref[...] = jnp.where(lane < FOX_DH, kn[0], kn[1])
    fvd = z[:, O_FV:O_FV + 256]
    fv_ref[...] = jnp.where(lane < FOX_DH, fvd[:, 0:128], fvd[:, 128:256])
    for hd in range(FOX_HEADS):
        n = hd // FOX_GROUP
        vall_ref[(MLA_HEADS + hd) * LANES:(MLA_HEADS + hd + 1) * LANES, :] = jnp.transpose(jnp.where(
            lane < FOX_DH, fvd[:, n * LANES:(n + 1) * LANES], one64)).astype(BF16)

    lfv = jnp.where((lane >= SM_FF) & (lane < SM_FF + FOX_HEADS), _logsig(small + fb_ref[...]), 0.0)
    lf_ref[...] = lfv[:, SM_FF:SM_FF + FOX_HEADS]

    if per_token:
        augq = augk = None
    else:
        if first:
            @pl.when(i % tiles_per_seq == 0)
            def _():
                carry_ref[...] = jnp.zeros_like(carry_ref)
        row = lax.broadcasted_iota(I32, (tm, tm), 0)
        col = lax.broadcasted_iota(I32, (tm, tm), 1)
        tril = jnp.where(row >= col, 1.0, 0.0).astype(BF16)
        cums = _dot(tril, jnp.concatenate(_split(lfv, 3), axis=1))
        fcum = cums[:, 0:128] + cums[:, 128:256] + cums[:, 256:384] + carry_ref[0:1, :]
        carry_ref[0:1, :] = fcum[tm - 1:tm, :]
        aug = _dot(jnp.concatenate(_split(fcum * LOG2E, 3), axis=1), epl_ref[...])
        augq = aug[:, 0:512] + oneq_ref[...]
        augk = aug[:, 512:1024] + onek_ref[...]

    for hd in range(FOX_HEADS):
        blk = z[:, O_FQ + hd * LANES:O_FQ + (hd + 1) * LANES]
        ms = jnp.sum(blk * blk, axis=-1, keepdims=True) * (1.0 / FOX_DH)
        qh = blk * lax.rsqrt(ms + EPS) * gfq_ref[...] * (FOX_SCALE * LOG2E)
        kh = jnp.where(lane < FOX_DH, kn[hd // FOX_GROUP], 0.0)
        if augq is not None:
            qh = qh + augq[:, hd * LANES:(hd + 1) * LANES]
            kh = kh + augk[:, hd * LANES:(hd + 1) * LANES]
        sl = slice((MLA_HEADS + hd) * LANES, (MLA_HEADS + hd + 1) * LANES)
        qall_ref[:, sl] = qh.astype(BF16)
        kall_ref[:, sl] = kh.astype(BF16)


def _rope_tables(pos):
    half = MLA_ROPE // 2
    inv = ROPE_THETA ** (-jnp.arange(half, dtype=F32) / half)
    ang = pos.astype(F32)[:, None] * inv[None, :]
    c, s = jnp.cos(ang), jnp.sin(ang)
    n = pos.shape[0]
    cos_t = jnp.concatenate([jnp.ones((n, 64), F32), c, c, jnp.zeros((n, 32), F32)], axis=1)
    sin_t = jnp.concatenate([jnp.zeros((n, 64), F32), -s, s, jnp.zeros((n, 32), F32)], axis=1)
    return cos_t, sin_t


def _lane_vec(pieces):
    v = jnp.zeros((LANES,), F32)
    for off, val in pieces:
        v = lax.dynamic_update_slice(v, val.astype(F32), (off,))
    return v[None, :]


def _aug_constants():
    e = np.zeros((3 * LANES, 2 * FOX_HEADS * LANES), np.float32)
    oneq = np.zeros((1, FOX_HEADS * LANES), np.float32)
    onek = np.zeros((1, FOX_HEADS * LANES), np.float32)
    for hd in range(FOX_HEADS):
        for j in range(3):
            e[j * LANES + SM_FF + hd, hd * LANES + FOX_DH + j] = 1.0
            e[j * LANES + SM_FF + hd, FOX_HEADS * LANES + hd * LANES + FOX_DH + 3 + j] = -1.0
            oneq[0, hd * LANES + FOX_DH + 3 + j] = 1.0
            onek[0, hd * LANES + FOX_DH + j] = 1.0
    return jnp.asarray(e, BF16), jnp.asarray(oneq), jnp.asarray(onek)


def _layer_weights(layer, w_in, gla_wa2, gla_ba, gla_onorm, mla_qlat_norm, mla_wuq, mla_q_norm, mla_kv_norm,
                   mla_kr_norm, mla_wuk, mla_wuv, fox_q_norm, fox_k_norm, fox_fb, attn_norm):
    w = w_in[layer]
    offs = np.concatenate([[0], np.cumsum(IN_SIZES)])
    gq, gk, gv, gg, gr, cq, ckv, kr, fq, fk, fv, ff = [w[:, offs[j]:offs[j + 1]] for j in range(12)]
    z64 = jnp.zeros((D_MODEL, 64), F32)
    fq_p = jnp.concatenate([t for hd in range(FOX_HEADS) for t in (fq[:, hd * 64:(hd + 1) * 64], z64)], axis=1)
    fk_p = jnp.concatenate([fk[:, 0:64], fk[:, 0:64], fk[:, 64:128], fk[:, 64:128]], axis=1)
    fv_p = jnp.concatenate([fv[:, 0:64], fv[:, 0:64], fv[:, 64:128], fv[:, 64:128]], axis=1)
    sm = jnp.concatenate([gg, ff, jnp.zeros((D_MODEL, SM_KR - SM_FF - FOX_HEADS), F32), kr,
                          jnp.zeros((D_MODEL, LANES - SM_KR - MLA_ROPE), F32)], axis=1)
    win_p = jnp.concatenate([gq, gk, gv, gr, cq, ckv, fq_p, fk_p, fv_p, sm], axis=1).astype(BF16)

    wuq = mla_wuq[layer].reshape(MLA_Q_RANK, MLA_HEADS, MLA_QK)
    wuq_p = jnp.concatenate([wuq, jnp.zeros((MLA_Q_RANK, MLA_HEADS, LANES - MLA_QK), F32)], axis=2)
    wuq_p = wuq_p.reshape(MLA_Q_RANK, MLA_HEADS * LANES).astype(BF16)
    wuk = mla_wuk[layer]
    wuk_p = jnp.concatenate([wuk, jnp.zeros_like(wuk)], axis=2).reshape(MLA_KV_RANK, MLA_HEADS * LANES).astype(BF16)
    wuv = mla_wuv[layer]
    wuv_p = jnp.concatenate([wuv, jnp.zeros_like(wuv)], axis=2).reshape(MLA_KV_RANK, MLA_HEADS * LANES).astype(BF16)
    wa2_p = jnp.zeros((LANES, LANES), F32).at[SM_GG:SM_GG + GLA_RANK].set(gla_wa2[layer]).astype(BF16)
    return dict(
        win=win_p, wuq=wuq_p, wuk=wuk_p, wuv=wuv_p, wa2=wa2_p,
        gat=attn_norm[layer][None, :], gql=mla_qlat_norm[layer][None, :], gkv=mla_kv_norm[layer][None, :],
        gqp=_lane_vec([(0, mla_q_norm[layer])]),
        gkr=_lane_vec([(SM_KR, mla_kr_norm[layer])]),
        gfq=_lane_vec([(0, fox_q_norm[layer])]),
        gfk=_lane_vec([(0, fox_k_norm[layer]), (FOX_DH, fox_k_norm[layer])]),
        fb=_lane_vec([(SM_FF, fox_fb[layer])]),
        ba=gla_ba[layer][None, :],
        onorm=jnp.tile(gla_onorm[layer], GLA_HEADS)[None, :],
        wuk_raw=wuk, wuv_raw=mla_wuv[layer],
    )


def _premix_call(x2d, shift, scale, cos_t, sin_t, lw, aug, *, tm, tiles_per_seq, per_token):
    t = x2d.shape[0]
    nt = t // tm
    epl, oneq, onek = aug
    full = lambda a: pl.BlockSpec(a.shape, lambda i: (0,) * a.ndim)
    tok = lambda w: pl.BlockSpec((tm, w), lambda i: (i, 0))
    if per_token:
        mod_spec = tok(D_MODEL)
        pos_spec = full(cos_t)
    else:
        mod_spec = pl.BlockSpec((1, 1, D_MODEL), lambda i: (i // tiles_per_seq, 0, 0))
        pos_spec = pl.BlockSpec((tm, LANES), lambda i: (i % tiles_per_seq, 0))
    consts = [lw["gat"], lw["win"], lw["gql"], lw["wuq"], lw["gqp"], lw["gkv"], lw["wuk"], lw["wuv"], lw["gkr"],
              lw["gfq"], lw["gfk"], lw["fb"], lw["wa2"], lw["ba"], epl, oneq, onek]
    out_widths = [(ATT_HEADS * LANES, BF16), (ATT_HEADS * LANES, BF16), (ATT_HEADS * LANES, BF16), (MLA_KV_RANK, F32),
                  (MLA_ROPE, F32), (LANES, F32), (LANES, F32), (FOX_HEADS, F32), (GLA_PACK, F32)]
    return pl.pallas_call(
        functools.partial(_premix_kernel, tiles_per_seq=tiles_per_seq, per_token=per_token,
                          n_sub=2 if tm % 256 == 0 else 1),
        grid=(nt,),
        in_specs=[tok(D_MODEL), mod_spec, mod_spec, pos_spec, pos_spec] + [full(c) for c in consts],
        out_specs=[tok(w) if j != 2 else pl.BlockSpec((1, w, tm), lambda i: (i // tiles_per_seq, 0, i % tiles_per_seq))
                   for j, (w, _) in enumerate(out_widths)],
        out_shape=[jax.ShapeDtypeStruct((t, w) if j != 2 else (nt // tiles_per_seq, w, tm * tiles_per_seq), dt)
                   for j, (w, dt) in enumerate(out_widths)],
        scratch_shapes=[pltpu.VMEM((8, LANES), F32)],
        compiler_params=_cparams(1), name="premix",
    )(x2d, shift, scale, cos_t, sin_t, *consts)


def _flash_kernel(q_ref, k_ref, vt_ref, o_ref, *, seq, tq, tk):
    for hh in range(2):
        for qb in range(seq // tq):
            q = q_ref[0, qb * tq:(qb + 1) * tq, hh * LANES:(hh + 1) * LANES]
            m = jnp.full((1, tq), -jnp.inf, F32)
            acc = jnp.zeros((LANES, tq), F32)
            for kc in range((qb + 1) * tq // tk):
                k = k_ref[0, kc * tk:(kc + 1) * tk, hh * LANES:(hh + 1) * LANES]
                vt = vt_ref[0, hh * LANES:(hh + 1) * LANES, kc * tk:(kc + 1) * tk]
                st = _dot_nt(k, q)
                if (kc + 1) * tk > qb * tq:
                    kr = lax.broadcasted_iota(I32, (tk, tq), 0) + kc * tk
                    qc = lax.broadcasted_iota(I32, (tk, tq), 1) + qb * tq
                    st = jnp.where(kr <= qc, st, -jnp.inf)
                m_new = jnp.maximum(m, jnp.max(st, axis=0, keepdims=True))
                p = jnp.exp2(st - m_new).astype(BF16)
                acc = jnp.exp2(m - m_new) * acc + _dot(vt, p)
                m = m_new
            o_t = acc[0:MLA_V, :] / acc[MLA_V:MLA_V + 1, :]
            o_ref[0, qb * tq:(qb + 1) * tq, hh * 64:(hh + 1) * 64] = jnp.transpose(o_t).astype(BF16)


def _flash_call(q_all, k_all, vt_all, *, tq, tk):
    b, s, _ = q_all.shape
    return pl.pallas_call(
        functools.partial(_flash_kernel, seq=s, tq=tq, tk=tk),
        grid=(b, ATT_HEADS // 2),
        in_specs=[pl.BlockSpec((1, s, 2 * LANES), lambda bi, p: (bi, 0, p)),
                  pl.BlockSpec((1, s, 2 * LANES), lambda bi, p: (bi, 0, p)),
                  pl.BlockSpec((1, 2 * LANES, s), lambda bi, p: (bi, p, 0))],
        out_specs=pl.BlockSpec((1, s, LANES), lambda bi, p: (bi, 0, p)),
        out_shape=jax.ShapeDtypeStruct((b, s, ATT_HEADS * 64), BF16),
        compiler_params=_cparams(2), name="flash_attn",
    )(q_all, k_all, vt_all)


def _gla_masks():
    r128 = lax.broadcasted_iota(I32, (GLA_HEADS * GLA_DK, GLA_HEADS * GLA_DV), 0)
    c256 = lax.broadcasted_iota(I32, (GLA_HEADS * GLA_DK, GLA_HEADS * GLA_DV), 1)
    return (r128 // GLA_DK) == (c256 // GLA_DV)


def _gla_out_norm(o, r, onorm):
    rr = lax.broadcasted_iota(I32, (256, 256), 0)
    cc = lax.broadcasted_iota(I32, (256, 256), 1)
    same = jnp.where((rr // GLA_DV) == (cc // GLA_DV), 1.0, 0.0).astype(BF16)
    ms = _dot((o * o).astype(BF16), same) * (1.0 / GLA_DV)
    return o * lax.rsqrt(ms + EPS) * onorm * _silu(r)


def _gla_kernel(g_ref, onorm_ref, o_ref, st_ref, sbd_ref, *, n_chunks, n_seq):
    si = pl.program_id(1)

    @pl.when(si == 0)
    def _():
        sbd_ref[...] = jnp.zeros_like(sbd_ref)

    c = GLA_CHUNK
    lane128 = lax.broadcasted_iota(I32, (1, 128), 1)
    lane256 = lax.broadcasted_iota(I32, (1, 256), 1)
    bd = _gla_masks()
    rr = lax.broadcasted_iota(I32, (c, c), 0)
    cc = lax.broadcasted_iota(I32, (c, c), 1)
    tril = jnp.where(rr >= cc, 1.0, 0.0).astype(BF16)
    r4 = lax.broadcasted_iota(I32, (GLA_HEADS * c, c), 0)
    c4 = lax.broadcasted_iota(I32, (GLA_HEADS * c, c), 1)
    causal4 = (r4 % c) >= c4
    e_r = lax.broadcasted_iota(I32, (128, 128), 0)
    e_c = lax.broadcasted_iota(I32, (128, 128), 1)
    eye = e_r == e_c

    for ci, bb in [(ci, bb) for ci in range(n_chunks) for bb in range(n_seq)]:
        blk = g_ref[bb, ci * c:(ci + 1) * c, :]
        q, k, la = blk[:, 0:128], blk[:, 128:256], blk[:, 256:384]
        v, r = blk[:, 384:640], blk[:, 640:896]
        b = _dot_split(tril, la, 2)
        b_end = b[c - 1:c, :]
        q_dec = q * jnp.exp(b)
        k_inv = (k * jnp.exp(-b)).astype(BF16)
        k_dec = (k * jnp.exp(b_end - b)).astype(BF16)
        vb = v.astype(BF16)
        s_bd = sbd_ref[bb]
        o = _dot(q_dec.astype(BF16), s_bd.astype(BF16))
        q4 = jnp.concatenate([jnp.where((lane128 // GLA_DK) == hd, q_dec, 0.0) for hd in range(GLA_HEADS)], axis=0)
        att = jnp.where(causal4, _dot_nt(q4.astype(BF16), k_inv), 0.0)
        r_all = _dot(att.astype(BF16), vb)
        for hd in range(GLA_HEADS):
            o = o + jnp.where((lane256 // GLA_DV) == hd, r_all[hd * c:(hd + 1) * c, :], 0.0)
        o_ref[bb, ci * c:(ci + 1) * c, :] = _gla_out_norm(o, r, onorm_ref[...]).astype(BF16)
        dcol = jnp.sum(jnp.where(eye, jnp.broadcast_to(jnp.exp(b_end), (128, 128)), 0.0), axis=1, keepdims=True)
        sbd_ref[bb] = s_bd * dcol + jnp.where(bd, _dot_tn(k_dec, vb), 0.0)

    @pl.when(si == pl.num_programs(1) - 1)
    def _():
        for bb in range(n_seq):
            s_bd = sbd_ref[bb]
            for hd in range(GLA_HEADS):
                st_ref[bb, hd] = s_bd[hd * GLA_DK:(hd + 1) * GLA_DK, hd * GLA_DV:(hd + 1) * GLA_DV]


def _gla_call(gla_in, onorm, *, sup):
    b, s, _ = gla_in.shape
    n_seq = 2 if b % 2 == 0 else 1
    return pl.pallas_call(
        functools.partial(_gla_kernel, n_chunks=sup // GLA_CHUNK, n_seq=n_seq),
        grid=(b // n_seq, s // sup),
        in_specs=[pl.BlockSpec((n_seq, sup, GLA_PACK), lambda bi, si: (bi, si, 0)),
                  pl.BlockSpec((1, 256), lambda bi, si: (0, 0))],
        out_specs=[pl.BlockSpec((n_seq, sup, 256), lambda bi, si: (bi, si, 0)),
                   pl.BlockSpec((n_seq, GLA_HEADS, GLA_DK, GLA_DV), lambda bi, si: (bi, 0, 0, 0))],
        out_shape=[jax.ShapeDtypeStruct((b, s, 256), BF16),
                   jax.ShapeDtypeStruct((b, GLA_HEADS, GLA_DK, GLA_DV), F32)],
        scratch_shapes=[pltpu.VMEM((n_seq, GLA_HEADS * GLA_DK, GLA_HEADS * GLA_DV), F32)],
        compiler_params=_cparams(2), name="gla_chunked",
    )(gla_in, onorm)


def _gla_step_kernel(g_ref, s0_ref, onorm_ref, o_ref, st_ref, *, nb):
    lane128 = lax.broadcasted_iota(I32, (1, 128), 1)
    lane256 = lax.broadcasted_iota(I32, (1, 256), 1)
    bd = _gla_masks()
    e_r = lax.broadcasted_iota(I32, (128, 128), 0)
    e_c = lax.broadcasted_iota(I32, (128, 128), 1)
    eye = e_r == e_c
    rows = []
    for j in range(nb):
        blk = g_ref[j:j + 1, :]
        q, k, la = blk[:, 0:128], blk[:, 128:256], blk[:, 256:384]
        v = blk[:, 384:640]
        s0 = s0_ref[j]
        s_st = s0.reshape(GLA_HEADS * GLA_DK, GLA_DV)
        s_bd = jnp.where(bd, jnp.concatenate([s_st] * GLA_HEADS, axis=1), 0.0)
        ea = jnp.exp(la)
        o = _dot((q * ea).astype(BF16), s_bd.astype(BF16))
        qk = q * k
        for hd in range(GLA_HEADS):
            dot_h = jnp.sum(jnp.where((lane128 // GLA_DK) == hd, qk, 0.0), axis=-1, keepdims=True)
            o = o + jnp.where((lane256 // GLA_DV) == hd, dot_h * v, 0.0)
        rows.append(o)
        ecol = jnp.sum(jnp.where(eye, jnp.broadcast_to(ea, (128, 128)), 0.0), axis=1, keepdims=True)
        kcol = jnp.sum(jnp.where(eye, jnp.broadcast_to(k, (128, 128)), 0.0), axis=1, keepdims=True)
        s_new = s_bd * ecol + jnp.where(bd, kcol * v, 0.0)
        for hd in range(GLA_HEADS):
            st_ref[j, hd] = s_new[hd * GLA_DK:(hd + 1) * GLA_DK, hd * GLA_DV:(hd + 1) * GLA_DV]
    o_all = jnp.concatenate(rows, axis=0)
    o_ref[...] = _gla_out_norm(o_all, g_ref[:, 640:896], onorm_ref[...]).astype(BF16)


def _gla_step_call(gla_in, state, onorm):
    t = gla_in.shape[0]
    nb = 8
    return pl.pallas_call(
        functools.partial(_gla_step_kernel, nb=nb),
        grid=(t // nb,),
        in_specs=[pl.BlockSpec((nb, GLA_PACK), lambda i: (i, 0)),
                  pl.BlockSpec((nb, GLA_HEADS, GLA_DK, GLA_DV), lambda i: (i, 0, 0, 0)),
                  pl.BlockSpec((1, 256), lambda i: (0, 0))],
        out_specs=[pl.BlockSpec((nb, 256), lambda i: (i, 0)),
                   pl.BlockSpec((nb, GLA_HEADS, GLA_DK, GLA_DV), lambda i: (i, 0, 0, 0))],
        out_shape=[jax.ShapeDtypeStruct((t, 256), BF16),
                   jax.ShapeDtypeStruct((t, GLA_HEADS, GLA_DK, GLA_DV), F32)],
        compiler_params=_cparams(1), name="gla_step",
    )(gla_in, state, onorm)


def _post_kernel(og_ref, oa_ref, wout_ref, x_ref, gm_ref, gffn_ref, sh_ref, sc_ref, rw_ref, rb_ref,
                 xmid_ref, t_ref, ew_ref, pos_ref, cnt_ref, *, per_token):
    tm = x_ref.shape[0]
    gm, sh, sc = (gm_ref[...], sh_ref[...], sc_ref[...]) if per_token else (gm_ref[0], sh_ref[0], sc_ref[0])
    mix = _dot(og_ref[...], wout_ref[0:256, :]) + _dot(oa_ref[...], wout_ref[256:1024, :])
    xm = x_ref[...] + gm * mix
    xmid_ref[...] = xm
    t = _rms(xm, gffn_ref[...]) * (1.0 + sc) + sh
    t_ref[...] = t.astype(BF16)

    t_hi, t_lo = _split(t, 2)
    logits = _dot(t_hi, rw_ref[0]) + _dot(t_hi, rw_ref[1]) + _dot(t_lo, rw_ref[0])
    lt = jnp.transpose(logits)[0:N_EXPERTS, :]
    scores = 1.0 / (1.0 + jnp.exp(-lt))
    biased = scores + rb_ref[...]
    sc_rows = [scores[e:e + 1, :] for e in range(N_EXPERTS)]
    b_rows = [biased[e:e + 1, :] for e in range(N_EXPERTS)]

    def first_max(vals):
        m = vals[0]
        for a in vals[1:]:
            m = jnp.maximum(m, a)
        idx = jnp.full(m.shape, len(vals) - 1, I32)
        for j in range(len(vals) - 2, -1, -1):
            idx = jnp.where(vals[j] == m, j, idx)
        return m, idx

    gsum, loc1, loc2 = [], [], []
    for g in range(N_GROUPS):
        a = b_rows[g * EXPERTS_PER_GROUP:(g + 1) * EXPERTS_PER_GROUP]
        m1, i1 = first_max(a)
        a2 = [jnp.where(i1 == j, -jnp.inf, a[j]) for j in range(EXPERTS_PER_GROUP)]
        m2, i2 = first_max(a2)
        gsum.append(m1 + m2)
        loc1.append(i1)
        loc2.append(i2)
    _, gsel = first_max(gsum)
    l1, l2 = loc1[N_GROUPS - 1], loc2[N_GROUPS - 1]
    for g in range(N_GROUPS - 2, -1, -1):
        l1 = jnp.where(gsel == g, loc1[g], l1)
        l2 = jnp.where(gsel == g, loc2[g], l2)
    e1 = gsel * EXPERTS_PER_GROUP + l1
    e2 = gsel * EXPERTS_PER_GROUP + l2
    w1 = jnp.zeros_like(sc_rows[0])
    w2 = jnp.zeros_like(sc_rows[0])
    for e in range(N_EXPERTS):
        w1 = jnp.where(e1 == e, sc_rows[e], w1)
        w2 = jnp.where(e2 == e, sc_rows[e], w2)
    den = w1 + w2
    ew_ref[0:1, :] = w1 / den
    ew_ref[1:2, :] = w2 / den

    sub = lax.broadcasted_iota(I32, (N_EXPERTS, tm), 0)
    rr = lax.broadcasted_iota(I32, (tm, tm), 0)
    cc = lax.broadcasted_iota(I32, (tm, tm), 1)
    triu = jnp.where(rr <= cc, 1.0, 0.0).astype(BF16)
    oh1 = jnp.where(sub == e1, 1.0, 0.0)
    oh2 = jnp.where(sub == e2, 1.0, 0.0)
    cum1 = _dot(oh1.astype(BF16), triu)
    cum2 = _dot(oh2.astype(BF16), triu)
    tot1 = cum1[:, tm - 1:tm]
    n8 = jnp.floor((tot1 + cum2[:, tm - 1:tm] + 7.0) * 0.125) * 8.0
    n8b = jnp.broadcast_to(n8, (N_EXPERTS, LANES))
    er = lax.broadcasted_iota(I32, (N_EXPERTS, N_EXPERTS), 0)
    ec = lax.broadcasted_iota(I32, (N_EXPERTS, N_EXPERTS), 1)
    off = _dot_split(jnp.where(ec < er, 1.0, 0.0).astype(BF16), n8b, 2)[:, 0:1]
    pos1 = jnp.sum(oh1 * (off + cum1 - oh1), axis=0, keepdims=True)
    pos2 = jnp.sum(oh2 * (off + tot1 + cum2 - oh2), axis=0, keepdims=True)
    pos_ref[0:1, :] = pos1.astype(I32)
    pos_ref[1:2, :] = pos2.astype(I32)
    cnt_ref[0] = n8b


def _post_call(o_gla, o_attn, w_out, x2d, gate, gffn, shift, scale, rw_pieces, rb_col, *, tm, tiles_per_seq, per_token):
    t = x2d.shape[0]
    nt = t // tm
    full = lambda a: pl.BlockSpec(a.shape, lambda i: (0,) * a.ndim)
    tok = lambda w: pl.BlockSpec((tm, w), lambda i: (i, 0))
    mod_spec = tok(D_MODEL) if per_token else pl.BlockSpec((1, 1, D_MODEL), lambda i: (i // tiles_per_seq, 0, 0))
    rowblk = pl.BlockSpec((2, tm), lambda i: (0, i))
    return pl.pallas_call(
        functools.partial(_post_kernel, per_token=per_token),
        grid=(nt,),
        in_specs=[tok(256), tok(768), full(w_out), tok(D_MODEL), mod_spec, full(gffn), mod_spec, mod_spec,
                  full(rw_pieces), full(rb_col)],
        out_specs=[tok(D_MODEL), tok(D_MODEL), rowblk, rowblk,
                   pl.BlockSpec((1, N_EXPERTS, LANES), lambda i: (i, 0, 0))],
        out_shape=[jax.ShapeDtypeStruct((t, D_MODEL), F32), jax.ShapeDtypeStruct((t, D_MODEL), BF16),
                   jax.ShapeDtypeStruct((2, t), F32), jax.ShapeDtypeStruct((2, t), I32),
                   jax.ShapeDtypeStruct((nt, N_EXPERTS, LANES), F32)],
        compiler_params=_cparams(1), name="post_mix_route",
    )(o_gla, o_attn, w_out, x2d, gate, gffn, shift, scale, rw_pieces, rb_col)


SEG_ALIGN = 8


def _local_rows(tm):
    return 2 * tm + N_EXPERTS * SEG_ALIGN


def _segment_copies(segn_ref, sego_ref, tile, local, sem, *, remote, tm, to_remote, wait):
    nbits = (2 * tm // SEG_ALIGN).bit_length()
    lo = 0
    for e in range(N_EXPERTS):
        n = segn_ref[tile * N_EXPERTS + e]
        g = sego_ref[tile * N_EXPERTS + e]
        for b in reversed(range(nbits)):
            size = SEG_ALIGN << b
            above = (n >> (b + 4)) << (b + 4)

            @pl.when((n & size) != 0)
            def _(lo=lo, g=g, above=above, size=size):
                lrows = local.at[pl.ds(pl.multiple_of(lo + above, SEG_ALIGN), size)]
                rrows = remote.at[pl.ds(pl.multiple_of(g + above, SEG_ALIGN), size)]
                cp = pltpu.make_async_copy(lrows, rrows, sem) if to_remote else pltpu.make_async_copy(rrows, lrows, sem)
                if wait:
                    cp.wait()
                else:
                    cp.start()
        lo = lo + n


def _dispatch_kernel(segn_ref, sego_ref, pos_ref, t_ref, xs_in_ref, xs_ref, stage, sem, *, tm):
    del xs_in_ref
    i = pl.program_id(0)
    n = pl.num_programs(0)
    slot = i % 2
    seg = functools.partial(_segment_copies, segn_ref, sego_ref, remote=xs_ref, tm=tm, to_remote=True)

    @pl.when(i > 0)
    def _():
        seg(i - 1, stage.at[1 - slot], sem.at[1 - slot], wait=True)

    nsl = _local_rows(tm)
    row = lax.broadcasted_iota(I32, (nsl, tm), 0)
    perm = jnp.where(row == pos_ref[0:1, :], 1.0, jnp.where(row == pos_ref[1:2, :], 1.0, 0.0)).astype(BF16)
    stage[slot] = _pack_rows(_dot(perm, t_ref[...]))
    seg(i, stage.at[slot], sem.at[slot], wait=False)

    @pl.when(i == n - 1)
    def _():
        seg(i, stage.at[slot], sem.at[slot], wait=True)


def _dispatch_call(seg_n, seg_off, pos, tb, n_slots, *, tm):
    t = tb.shape[0]
    zeros = jnp.zeros((n_slots, 512), U32)
    grid_spec = pltpu.PrefetchScalarGridSpec(
        num_scalar_prefetch=2, grid=(t // tm,),
        in_specs=[pl.BlockSpec((2, tm), lambda i, sn, so: (0, i)),
                  pl.BlockSpec((tm, D_MODEL), lambda i, sn, so: (i, 0)),
                  pl.BlockSpec(memory_space=pl.ANY)],
        out_specs=pl.BlockSpec(memory_space=pl.ANY),
        scratch_shapes=[pltpu.VMEM((2, _local_rows(tm), 512), U32), pltpu.SemaphoreType.DMA((2,))])
    return pl.pallas_call(
        functools.partial(_dispatch_kernel, tm=tm), grid_spec=grid_spec,
        out_shape=jax.ShapeDtypeStruct((n_slots, 512), U32),
        input_output_aliases={4: 0},
        compiler_params=_cparams(1), name="moe_dispatch",
    )(seg_n, seg_off, pos, tb, zeros)


def _expert_kernel(be_ref, nu_ref, xs_ref, w1_ref, w3_ref, w2_ref, ys_ref):
    j = pl.program_id(0)

    @pl.when(j < nu_ref[0])
    def _():
        xa, xb = _unpack_rows(xs_ref[...])
        xa, xb = xa.astype(BF16), xb.astype(BF16)
        h1 = _dot(xa, w1_ref[0, 0:512, :]) + _dot(xb, w1_ref[0, 512:1024, :])
        h3 = _dot(xa, w3_ref[0, 0:512, :]) + _dot(xb, w3_ref[0, 512:1024, :])
        y = _dot((_silu(h1) * h3).astype(BF16), w2_ref[0])
        ys_ref[...] = _pack_rows(y)

    @pl.when(j >= nu_ref[0])
    def _():
        ys_ref[...] = jnp.zeros_like(ys_ref)


def _expert_call(block_expert, n_used, xs, w1, w3, w2):
    n_slots = xs.shape[0]
    nb = n_slots // MOE_BLK
    grid_spec = pltpu.PrefetchScalarGridSpec(
        num_scalar_prefetch=2, grid=(nb,),
        in_specs=[pl.BlockSpec((MOE_BLK, 512), lambda j, be, nu: (j, 0)),
                  pl.BlockSpec((1, D_MODEL, D_EXPERT), lambda j, be, nu: (be[j], 0, 0)),
                  pl.BlockSpec((1, D_MODEL, D_EXPERT), lambda j, be, nu: (be[j], 0, 0)),
                  pl.BlockSpec((1, D_EXPERT, D_MODEL), lambda j, be, nu: (be[j], 0, 0))],
        out_specs=pl.BlockSpec((MOE_BLK, 512), lambda j, be, nu: (j, 0)))
    return pl.pallas_call(
        _expert_kernel, grid_spec=grid_spec,
        out_shape=jax.ShapeDtypeStruct((n_slots, 512), U32),
        compiler_params=_cparams(1), name="moe_experts",
    )(block_expert, n_used, xs, w1, w3, w2)


def _combine_kernel(segn_ref, sego_ref, ys_ref, xmid_ref, gc_ref, wcol_ref, pcol_ref, o_ref, ybuf, sem, *, tm, per_token):
    i = pl.program_id(0)
    n = pl.num_programs(0)
    slot = i % 2
    seg = functools.partial(_segment_copies, segn_ref, sego_ref, remote=ys_ref, tm=tm, to_remote=False)

    @pl.when(i == 0)
    def _():
        ybuf[...] = jnp.zeros_like(ybuf)
        seg(0, ybuf.at[0], sem.at[0], wait=False)

    @pl.when(i + 1 < n)
    def _():
        seg(i + 1, ybuf.at[1 - slot], sem.at[1 - slot], wait=False)

    seg(i, ybuf.at[slot], sem.at[slot], wait=True)

    nsl = _local_rows(tm)
    lane = lax.broadcasted_iota(I32, (tm, nsl), 1)
    w, p = wcol_ref[...], pcol_ref[...]
    wm = jnp.where(lane == p[:, 0:1], w[:, 0:1], 0.0) + jnp.where(lane == p[:, 1:2], w[:, 1:2], 0.0)
    w_hi, w_lo = _split(wm, 2)
    ya, yb = _unpack_rows(ybuf[slot])
    ya, yb = ya.astype(BF16), yb.astype(BF16)
    gc = gc_ref[...] if per_token else gc_ref[0]
    o_ref[:, 0:512] = xmid_ref[:, 0:512] + gc[:, 0:512] * (_dot(w_hi, ya) + _dot(w_lo, ya))
    o_ref[:, 512:1024] = xmid_ref[:, 512:1024] + gc[:, 512:1024] * (_dot(w_hi, yb) + _dot(w_lo, yb))


def _combine_call(seg_n, seg_off, ys, xmid, gate, wcol, pcol, *, tm, tiles_per_seq, per_token):
    t = xmid.shape[0]
    mod_spec = (pl.BlockSpec((tm, D_MODEL), lambda i, sn, so: (i, 0)) if per_token
                else pl.BlockSpec((1, 1, D_MODEL), lambda i, sn, so: (i // tiles_per_seq, 0, 0)))
    grid_spec = pltpu.PrefetchScalarGridSpec(
        num_scalar_prefetch=2, grid=(t // tm,),
        in_specs=[pl.BlockSpec(memory_space=pl.ANY),
                  pl.BlockSpec((tm, D_MODEL), lambda i, sn, so: (i, 0)), mod_spec,
                  pl.BlockSpec((tm, 2), lambda i, sn, so: (i, 0)),
                  pl.BlockSpec((tm, 2), lambda i, sn, so: (i, 0))],
        out_specs=pl.BlockSpec((tm, D_MODEL), lambda i, sn, so: (i, 0)),
        scratch_shapes=[pltpu.VMEM((2, _local_rows(tm), 512), U32), pltpu.SemaphoreType.DMA((2,))])
    return pl.pallas_call(
        functools.partial(_combine_kernel, tm=tm, per_token=per_token), grid_spec=grid_spec,
        out_shape=jax.ShapeDtypeStruct((t, D_MODEL), F32),
        compiler_params=_cparams(1), name="moe_combine",
    )(seg_n, seg_off, ys, xmid, gate, wcol, pcol)


def _moe(tb, ew, pos, cnt, xmid, gate, w1, w3, w2, *, tm, tiles_per_seq, per_token):
    t = xmid.shape[0]
    nt = t // tm
    n8 = cnt[:, :, 0].astype(I32)
    total = jnp.sum(n8, axis=0)
    padded = (total + MOE_BLK - 1) // MOE_BLK * MOE_BLK
    pend = jnp.cumsum(padded)
    seg_off = ((pend - padded)[None, :] + jnp.cumsum(n8, axis=0) - n8).astype(I32).reshape(-1)
    seg_n = n8.reshape(-1)
    nb = -(-(2 * t + nt * N_EXPERTS * (SEG_ALIGN - 1)) // MOE_BLK) + N_EXPERTS
    block_expert = jnp.clip(jnp.searchsorted(pend, jnp.arange(nb, dtype=I32) * MOE_BLK, side="right"),
                            0, N_EXPERTS - 1).astype(I32)
    n_used = (pend[-1:] // MOE_BLK).astype(I32)
    xs = _dispatch_call(seg_n, seg_off, pos, tb, nb * MOE_BLK, tm=tm)
    ys = _expert_call(block_expert, n_used, xs, w1, w3, w2)
    return _combine_call(seg_n, seg_off, ys, xmid, gate, jnp.transpose(ew), jnp.transpose(pos), tm=tm,
                         tiles_per_seq=tiles_per_seq, per_token=per_token)


def _qprep_kernel(qall_ref, wukt_ref, ql_ref, qr_ref, qf_ref):
    for hd in range(MLA_HEADS):
        blk = qall_ref[:, hd * LANES:(hd + 1) * LANES]
        ql_ref[:, hd, :] = _dot(blk, wukt_ref[hd])
        qr_ref[:, hd, :] = blk[:, MLA_NOPE:MLA_QK].astype(F32)
    for hd in range(FOX_HEADS):
        qf_ref[:, hd, :] = qall_ref[:, (MLA_HEADS + hd) * LANES:(MLA_HEADS + hd) * LANES + FOX_DH].astype(F32)


def _qprep_call(q_all, wukt_p):
    t = q_all.shape[0]
    return pl.pallas_call(
        _qprep_kernel,
        out_shape=[jax.ShapeDtypeStruct((t, MLA_HEADS, MLA_KV_RANK), F32),
                   jax.ShapeDtypeStruct((t, MLA_HEADS, MLA_ROPE), F32),
                   jax.ShapeDtypeStruct((t, FOX_HEADS, FOX_DH), F32)],
        compiler_params=pltpu.CompilerParams(vmem_limit_bytes=VMEM_LIMIT), name="sample_qprep",
    )(q_all, wukt_p)


def _foxbias_kernel(pt_ref, lf_hbm, o_ref, buf, sem, *, layer, n_pages):
    b = pl.program_id(0)
    nb = pl.num_programs(0)

    def copies(bb, slot):
        return [pltpu.make_async_copy(lf_hbm.at[layer, pt_ref[bb, p]], buf.at[slot, p], sem.at[slot])
                for p in range(n_pages)]

    @pl.when(b == 0)
    def _():
        for cp in copies(0, 0):
            cp.start()

    slot = b % 2

    @pl.when(b + 1 < nb)
    def _():
        for cp in copies(b + 1, 1 - slot):
            cp.start()

    for cp in copies(b, slot):
        cp.wait()

    rr = lax.broadcasted_iota(I32, (PAGE_SIZE, PAGE_SIZE), 0)
    cc = lax.broadcasted_iota(I32, (PAGE_SIZE, PAGE_SIZE), 1)
    m_suf = jnp.where(rr > cc, 1.0, 0.0).astype(BF16)
    m_tot = jnp.ones((PAGE_SIZE, PAGE_SIZE), BF16)
    pr = lax.broadcasted_iota(I32, (n_pages, n_pages), 0)
    pc = lax.broadcasted_iota(I32, (n_pages, n_pages), 1)
    later = jnp.where(pc > pr, 1.0, 0.0).astype(BF16)
    for hd in range(FOX_HEADS):
        pieces = _split(buf[slot, :, hd, :], 3)
        within = sum(_dot(p, m_suf) for p in pieces)
        total = sum(_dot(p, m_tot) for p in pieces)
        o_ref[0, hd] = within + _dot_split(later, total, 3)


def _foxbias_call(page_table, lf_cache, layer):
    nb, n_pages = page_table.shape
    grid_spec = pltpu.PrefetchScalarGridSpec(
        num_scalar_prefetch=1, grid=(nb,),
        in_specs=[pl.BlockSpec(memory_space=pl.ANY)],
        out_specs=pl.BlockSpec((1, FOX_HEADS, n_pages, PAGE_SIZE), lambda b, pt: (b, 0, 0, 0)),
        scratch_shapes=[pltpu.VMEM((2, n_pages, FOX_HEADS, PAGE_SIZE), F32), pltpu.SemaphoreType.DMA((2,))])
    return pl.pallas_call(
        functools.partial(_foxbias_kernel, layer=layer, n_pages=n_pages), grid_spec=grid_spec,
        out_shape=jax.ShapeDtypeStruct((nb, FOX_HEADS, n_pages, PAGE_SIZE), F32),
        compiler_params=_cparams(1), name="fox_past_bias",
    )(page_table, lf_cache)


def _paged_kernel(pt_ref, ql_ref, qr_ref, qf_ref, bias_ref, gnew_ref, ckvn_ref, krn_ref, fkn_ref, fvn_ref,
                  ckv_hbm, kr_hbm, fk_hbm, fv_hbm, ol_ref, of_ref,
                  ckv_buf, kr_buf, fk_buf, fv_buf, sem, m_ref, l_ref, acc_ref, mf_ref, lf_ref, accf_ref,
                  *, layer, n_chunks, pg):
    s = pl.program_id(0)
    ns = pl.num_programs(0)
    chunk = s % n_chunks

    def copies(step, slot):
        bb, ch = step // n_chunks, step % n_chunks
        out = []
        for p in range(pg):
            phys = pt_ref[bb, ch * pg + p]
            keys = pl.ds(p * PAGE_SIZE, PAGE_SIZE)
            out.append(pltpu.make_async_copy(ckv_hbm.at[layer, phys], ckv_buf.at[slot, keys], sem.at[slot, 0]))
            out.append(pltpu.make_async_copy(kr_hbm.at[layer, phys], kr_buf.at[slot, :, keys], sem.at[slot, 1]))
            out.append(pltpu.make_async_copy(fk_hbm.at[layer, phys], fk_buf.at[slot, :, :, keys], sem.at[slot, 2]))
            out.append(pltpu.make_async_copy(fv_hbm.at[layer, phys], fv_buf.at[slot, :, :, keys], sem.at[slot, 3]))
        return out

    @pl.when(s == 0)
    def _():
        for cp in copies(0, 0):
            cp.start()

    slot = s % 2

    @pl.when(s + 1 < ns)
    def _():
        for cp in copies(s + 1, 1 - slot):
            cp.start()

    @pl.when(chunk == 0)
    def _():
        m_ref[...] = jnp.full_like(m_ref, -1e30)
        l_ref[...] = jnp.zeros_like(l_ref)
        acc_ref[...] = jnp.zeros_like(acc_ref)
        mf_ref[...] = jnp.full_like(mf_ref, -1e30)
        lf_ref[...] = jnp.zeros_like(lf_ref)
        accf_ref[...] = jnp.zeros_like(accf_ref)

    for cp in copies(s, slot):
        cp.wait()

    def merge(m_r, l_r, acc_r, sc, pv):
        m_old = m_r[:, 0:1]
        m_new = jnp.maximum(m_old, jnp.max(sc, axis=-1, keepdims=True))
        alpha = jnp.exp2(m_old - m_new)
        p = jnp.exp2(sc - m_new)
        l_r[...] = jnp.broadcast_to(alpha * l_r[:, 0:1] + jnp.sum(p, axis=-1, keepdims=True), l_r.shape)
        acc_r[...] = alpha * acc_r[...] + pv(p)
        m_r[...] = jnp.broadcast_to(m_new, m_r.shape)

    ql, qr, qf = ql_ref[0], qr_ref[0], qf_ref[0]
    ckv = ckv_buf[slot]
    merge(m_ref, l_ref, acc_ref, _dot_nt(ql, ckv) + _dot(qr, kr_buf[slot]), lambda p: _dot(p, ckv))
    sf = jnp.concatenate([_dot(qf[n * FOX_GROUP:(n + 1) * FOX_GROUP], fk_buf[slot, n])
                          for n in range(FOX_KV_HEADS)], axis=0)
    merge(mf_ref, lf_ref, accf_ref, sf + (bias_ref[0] + gnew_ref[0]) * LOG2E,
          lambda p: jnp.concatenate([_dot_nt(p[n * FOX_GROUP:(n + 1) * FOX_GROUP], fv_buf[slot, n])
                                     for n in range(FOX_KV_HEADS)], axis=0))

    @pl.when(chunk == n_chunks - 1)
    def _():
        ckvn = ckvn_ref[0]
        s_new = jnp.sum(ql * ckvn, axis=-1, keepdims=True) + jnp.sum(qr * krn_ref[0], axis=-1, keepdims=True)
        merge(m_ref, l_ref, acc_ref, s_new, lambda p: p * ckvn)
        ol_ref[0] = acc_ref[...] / l_ref[:, 0:1]
        rep = lambda a: jnp.concatenate([a[:, n * FOX_DH:(n + 1) * FOX_DH] for n in range(FOX_KV_HEADS)
                                         for _ in range(FOX_GROUP)], axis=0)
        fkn, fvn = rep(fkn_ref[0]), rep(fvn_ref[0])
        sf_new = jnp.sum(qf * fkn, axis=-1, keepdims=True)
        merge(mf_ref, lf_ref, accf_ref, sf_new, lambda p: p * fvn)
        of_ref[0] = accf_ref[...] / lf_ref[:, 0:1]


def _paged_call(page_table, ql, qr, qf, bias, gnew, ckvn, krn, fkn, fvn, c_ckv, c_kr, c_fk, c_fv, layer, *, pg):
    nb, n_pages = page_table.shape
    n_chunks = n_pages // pg
    kc = pg * PAGE_SIZE
    per_b = lambda shp: pl.BlockSpec((1,) + shp, lambda s, pt: (s // n_chunks,) + (0,) * len(shp))
    grid_spec = pltpu.PrefetchScalarGridSpec(
        num_scalar_prefetch=1, grid=(nb * n_chunks,),
        in_specs=[per_b((MLA_HEADS, MLA_KV_RANK)), per_b((MLA_HEADS, MLA_ROPE)), per_b((FOX_HEADS, FOX_DH)),
                  pl.BlockSpec((1, FOX_HEADS, kc), lambda s, pt: (s // n_chunks, 0, s % n_chunks)),
                  per_b((FOX_HEADS, 1)), per_b((1, MLA_KV_RANK)), per_b((1, MLA_ROPE)), per_b((1, LANES)),
                  per_b((1, LANES))] + [pl.BlockSpec(memory_space=pl.ANY)] * 4,
        out_specs=[per_b((MLA_HEADS, MLA_KV_RANK)), per_b((FOX_HEADS, FOX_DH))],
        scratch_shapes=[pltpu.VMEM((2, kc, MLA_KV_RANK), F32), pltpu.VMEM((2, MLA_ROPE, kc), F32),
                        pltpu.VMEM((2, FOX_KV_HEADS, FOX_DH, kc), F32), pltpu.VMEM((2, FOX_KV_HEADS, FOX_DH, kc), F32),
                        pltpu.SemaphoreType.DMA((2, 4)),
                        pltpu.VMEM((MLA_HEADS, LANES), F32), pltpu.VMEM((MLA_HEADS, LANES), F32),
                        pltpu.VMEM((MLA_HEADS, MLA_KV_RANK), F32),
                        pltpu.VMEM((FOX_HEADS, LANES), F32), pltpu.VMEM((FOX_HEADS, LANES), F32),
                        pltpu.VMEM((FOX_HEADS, FOX_DH), F32)])
    return pl.pallas_call(
        functools.partial(_paged_kernel, layer=layer, n_chunks=n_chunks, pg=pg), grid_spec=grid_spec,
        out_shape=[jax.ShapeDtypeStruct((nb, MLA_HEADS, MLA_KV_RANK), F32),
                   jax.ShapeDtypeStruct((nb, FOX_HEADS, FOX_DH), F32)],
        compiler_params=_cparams(1), name="paged_attn",
    )(page_table, ql, qr, qf, bias, gnew, ckvn, krn, fkn, fvn, c_ckv, c_kr, c_fk, c_fv)


def _oproj_kernel(ol_ref, of_ref, wuv_ref, o_ref):
    for hd in range(MLA_HEADS):
        o_ref[:, hd * MLA_V:(hd + 1) * MLA_V] = _dot(ol_ref[:, hd, :].astype(BF16), wuv_ref[hd]).astype(BF16)
    for hd in range(FOX_HEADS):
        o_ref[:, 512 + hd * FOX_DH:512 + (hd + 1) * FOX_DH] = of_ref[:, hd, :].astype(BF16)


def _oproj_call(o_lat, o_fox, wuv_h):
    t = o_lat.shape[0]
    return pl.pallas_call(
        _oproj_kernel, out_shape=jax.ShapeDtypeStruct((t, ATT_HEADS * 64), BF16),
        compiler_params=pltpu.CompilerParams(vmem_limit_bytes=VMEM_LIMIT), name="sample_oproj",
    )(o_lat, o_fox, wuv_h)


def kernel(x_prompt, x_sample, cache_mla_ckv, cache_mla_krope, cache_fox_k, cache_fox_v, cache_fox_logf, state_gla, page_table, c_prompt, c_sample, router_w, router_b, attn_norm, ffn_norm, ada_w, ada_b, w_in, gla_wa2, gla_ba, gla_onorm, mla_qlat_norm, mla_wuq, mla_q_norm, mla_kv_norm, mla_kr_norm, mla_wuk, mla_wuv, fox_q_norm, fox_k_norm, fox_fb, w_out, moe_w1, moe_w3, moe_w2):
    bp, seq, _ = x_prompt.shape
    bs = x_sample.shape[0]
    depth = w_in.shape[0]
    n_pages = page_table.shape[1]
    tp = bp * seq
    tm_p = min(TM_PROMPT, seq)
    tps = seq // tm_p
    tm_m = min(TM_MOE, seq)
    tps_m = seq // tm_m
    tq = min(TQ, seq)
    sup = min(GLA_SUPER, seq)
    pg = min(PAGES_PER_STEP, n_pages)

    mods = _ada_call(jnp.concatenate([c_prompt, c_sample], axis=0), ada_w, ada_b)
    cos_p, sin_p = _rope_tables(jnp.arange(seq))
    cos_s, sin_s = _rope_tables(jnp.full((1,), n_pages * PAGE_SIZE))
    aug = _aug_constants()
    rw_pad = jnp.zeros((D_MODEL, LANES), F32).at[:, :N_EXPERTS].set(router_w)
    rw_hi = rw_pad.astype(BF16)
    rw_pieces = jnp.stack([rw_hi, (rw_pad - rw_hi.astype(F32)).astype(BF16)])
    rb_col = router_b.astype(F32)[:, None]
    c_kr = jnp.transpose(cache_mla_krope, (0, 1, 3, 2))
    c_fk = jnp.transpose(cache_fox_k, (0, 1, 3, 4, 2))
    c_fv = jnp.transpose(cache_fox_v, (0, 1, 3, 4, 2))
    c_lf = jnp.transpose(cache_fox_logf, (0, 1, 3, 2))

    xp = x_prompt.reshape(tp, D_MODEL)
    xs = x_sample.reshape(bs, D_MODEL)
    p_ent, s_ent = [], []
    for layer in range(depth):
        lw = _layer_weights(layer, w_in, gla_wa2, gla_ba, gla_onorm, mla_qlat_norm, mla_wuq, mla_q_norm,
                            mla_kv_norm, mla_kr_norm, mla_wuk, mla_wuv, fox_q_norm, fox_k_norm, fox_fb, attn_norm)
        wout_b = w_out[layer].astype(BF16)
        w1b, w3b, w2b = moe_w1[layer].astype(BF16), moe_w3[layer].astype(BF16), moe_w2[layer].astype(BF16)
        gffn = ffn_norm[layer][None, :]
        mod = mods[layer]
        mp = [mod[:bp, j * D_MODEL:(j + 1) * D_MODEL].reshape(bp, 1, D_MODEL) for j in range(6)]
        ms = [mod[bp:, j * D_MODEL:(j + 1) * D_MODEL] for j in range(6)]

        (q_all, k_all, v_all, ckv, kr, fk, fv, lf, gla_in) = _premix_call(
            xp, mp[0], mp[1], cos_p, sin_p, lw, aug, tm=tm_p, tiles_per_seq=tps, per_token=False)
        o_gla, st = _gla_call(gla_in.reshape(bp, seq, GLA_PACK), lw["onorm"], sup=sup)
        o_att = _flash_call(q_all.reshape(bp, seq, -1), k_all.reshape(bp, seq, -1), v_all, tq=tq, tk=min(TK, seq))
        p_ent.append((ckv.reshape(bp, seq, MLA_KV_RANK), kr.reshape(bp, seq, MLA_ROPE),
                      fk.reshape(bp, seq, FOX_KV_HEADS, FOX_DH), fv.reshape(bp, seq, FOX_KV_HEADS, FOX_DH),
                      lf.reshape(bp, seq, FOX_HEADS), st))
        xmid, tb, ew, pos, cnt = _post_call(
            o_gla.reshape(tp, 256), o_att.reshape(tp, -1), wout_b, xp, mp[2], gffn, mp[3], mp[4], rw_pieces, rb_col,
            tm=tm_m, tiles_per_seq=tps_m, per_token=False)
        xp = _moe(tb, ew, pos, cnt, xmid, mp[5], w1b, w3b, w2b, tm=tm_m, tiles_per_seq=tps_m, per_token=False)

        (q_all, _, _, ckv, kr, fk, fv, lf, gla_in) = _premix_call(
            xs, ms[0], ms[1], cos_s, sin_s, lw, aug, tm=bs, tiles_per_seq=1, per_token=True)
        o_gla, st = _gla_step_call(gla_in, state_gla[layer], lw["onorm"])
        wukt_p = jnp.concatenate([jnp.transpose(lw["wuk_raw"], (1, 2, 0)),
                                  jnp.zeros((MLA_HEADS, LANES - MLA_NOPE, MLA_KV_RANK), F32)], axis=1).astype(BF16)
        ql, qr, qf = _qprep_call(q_all, wukt_p)
        bias = _foxbias_call(page_table, c_lf, layer).reshape(bs, FOX_HEADS, n_pages * PAGE_SIZE)
        o_lat, o_fox = _paged_call(page_table, ql, qr, qf, bias, lf.reshape(bs, FOX_HEADS, 1),
                                   ckv.reshape(bs, 1, -1), kr.reshape(bs, 1, -1), fk.reshape(bs, 1, -1),
                                   fv.reshape(bs, 1, -1), cache_mla_ckv, c_kr, c_fk, c_fv, layer, pg=pg)
        o_att = _oproj_call(o_lat, o_fox, jnp.transpose(lw["wuv_raw"], (1, 0, 2)).astype(BF16))
        s_ent.append((ckv.reshape(bs, 1, MLA_KV_RANK), kr.reshape(bs, 1, MLA_ROPE),
                      fk.reshape(bs, 1, FOX_KV_HEADS, FOX_DH), fv.reshape(bs, 1, FOX_KV_HEADS, FOX_DH),
                      lf.reshape(bs, 1, FOX_HEADS), st))
        xmid, tb, ew, pos, cnt = _post_call(
            o_gla, o_att, wout_b, xs, ms[2], gffn, ms[3], ms[4], rw_pieces, rb_col,
            tm=bs, tiles_per_seq=1, per_token=True)
        xs = _moe(tb, ew, pos, cnt, xmid, ms[5], w1b, w3b, w2b, tm=bs, tiles_per_seq=1, per_token=True)

    stack = lambda ents, j: jnp.stack([e[j] for e in ents], axis=0)
    return (xp.reshape(bp, seq, D_MODEL), xs.reshape(bs, 1, D_MODEL),
            stack(p_ent, 0), stack(p_ent, 1), stack(p_ent, 2), stack(p_ent, 3), stack(p_ent, 4), stack(p_ent, 5),
            stack(s_ent, 0), stack(s_ent, 1), stack(s_ent, 2), stack(s_ent, 3), stack(s_ent, 4), stack(s_ent, 5))
```
